```python
import jax, jax.numpy as jnp
from jax import lax
import numpy as np

D_MODEL = 1024
BATCH = 16
SEQ = 256
DEPTH = 4
DEC_BATCH = 8
DEC_SEQ = 2048
PAST_LEN = 512

GRID_W = 64
N_MIXERS = 2
N_REC_LAYERS = (DEPTH + 1) // 2
N_ATTN_LAYERS = DEPTH // 2
N_MOD = 6
EPS = 1e-6
NEG_INF = -1e30
REC_HEADS = 8
REC_DK = D_MODEL // REC_HEADS
REC_DV = D_MODEL // REC_HEADS
REC_CHUNK = 64
ATTN_HEADS = 16
ATTN_KV_HEADS = 4
ATTN_GROUPS = ATTN_HEADS // ATTN_KV_HEADS
HEAD_DIM = D_MODEL // ATTN_HEADS
WINDOW = 128
ATTN_SCALE = HEAD_DIM ** -0.5
ROPE_BASE = 10000.0
PEER_HEADS = 8
PEER_NKEYS = 128
PEER_EXPERTS = PEER_NKEYS * PEER_NKEYS
PEER_TOPK = 16
PEER_DQ = 256
PEER_DQ_HALF = PEER_DQ // 2
PEER_TOKEN_BLOCK = 128

kernel_name = 'hgrn2_swa_peer_diffusion_step'


def _rms_norm(x, w):
    xf = x.astype(jnp.float32)
    y = xf * lax.rsqrt(jnp.mean(xf * xf, axis=-1, keepdims=True) + EPS)
    return (y * w.astype(jnp.float32)).astype(x.dtype)


def _modulate(h, shift, scale):
    return h * (1 + scale) + shift


def _rope_2d(x):
    L = x.shape[1]
    rows = L // GRID_W
    row = jnp.repeat(jnp.arange(rows), GRID_W).astype(jnp.float32)
    col = (jnp.arange(L) % GRID_W).astype(jnp.float32)
    half = HEAD_DIM // 2
    nf = half // 2
    freqs = ROPE_BASE ** (-jnp.arange(nf, dtype=jnp.float32) / nf)

    def rot(xa, pos):
        ang = pos[:, None] * freqs[None, :]
        cos = jnp.cos(ang)[None, :, None, :]
        sin = jnp.sin(ang)[None, :, None, :]
        x1, x2 = xa[..., :nf], xa[..., nf:]
        return jnp.concatenate([x1 * cos - x2 * sin, x1 * sin + x2 * cos], axis=-1)

    xf = x.astype(jnp.float32)
    return jnp.concatenate([rot(xf[..., :half], row), rot(xf[..., half:], col)], axis=-1).astype(x.dtype)


def _gla_scan(q, k, v, logf, s0):
    B, L, H, _ = q.shape
    n = L // REC_CHUNK

    def chunks(t):
        return jnp.moveaxis(t.reshape(B, n, REC_CHUNK, H, t.shape[-1]), 1, 0)

    causal = jnp.tril(jnp.ones((REC_CHUNK, REC_CHUNK), dtype=bool))[None, :, :, None, None]

    def step(S, inp):
        qc, kc, vc, gc = inp
        b = jnp.cumsum(gc, axis=1)
        o_inter = jnp.einsum('bthk,bhkv->bthv', qc * jnp.exp(b), S)
        decay = jnp.exp(jnp.where(causal, b[:, :, None] - b[:, None, :], -jnp.inf))
        scores = jnp.einsum('btshk,bshk->bhts', qc[:, :, None] * decay, kc)
        o_intra = jnp.einsum('bhts,bshv->bthv', scores, vc)
        b_end = b[:, -1]
        S_new = jnp.exp(b_end)[..., None] * S + jnp.einsum('bshk,bshv->bhkv', kc * jnp.exp(b_end[:, None] - b), vc)
        return S_new, o_inter + o_intra

    S_fin, o = lax.scan(step, s0, (chunks(q), chunks(k), chunks(v), chunks(logf)))
    return jnp.moveaxis(o, 0, 1).reshape(B, L, H, -1), S_fin


def _rec_project(h, w_in, lb):
    B, L, _ = h.shape
    z = (h @ w_in).astype(jnp.float32)
    q, f_fwd, f_bwd, i, g = jnp.split(z, 5, axis=-1)
    heads = lambda t: t.reshape(B, L, REC_HEADS, -1)

    def forget(fz, lbd):
        log_f = jnp.logaddexp(jnp.log(lbd), jnp.log1p(-lbd) + jax.nn.log_sigmoid(fz))
        k = (1.0 - lbd) * jax.nn.sigmoid(-fz)
        return heads(k), heads(log_f)

    return heads(jax.nn.silu(q)), heads(i), forget(f_fwd, lb[0]), forget(f_bwd, lb[1]), heads(g)


def _bidir_scan(q, v, fwd, bwd, s0_fwd, s0_bwd):
    flip = lambda t: jnp.flip(t, axis=1)
    o_f, s_f = _gla_scan(q, fwd[0], v, fwd[1], s0_fwd)
    o_b, s_b = _gla_scan(flip(q), flip(bwd[0]), flip(v), flip(bwd[1]), s0_bwd)
    return o_f + flip(o_b), s_f, s_b


def _rec_mixer(hp, hs, s0_lat, w_in, lb, gn_w, w_out):
    def readout(o, g, dtype):
        B, L = o.shape[:2]
        y = _rms_norm(o, gn_w) * jax.nn.silu(g)
        return y.reshape(B, L, D_MODEL).astype(dtype) @ w_out

    q, v, fwd, bwd, g = _rec_project(hp, w_in, lb)
    zeros = jnp.zeros((hp.shape[0], REC_HEADS, REC_DK, REC_DV), jnp.float32)
    o, s_f, s_b = _bidir_scan(q, v, fwd, bwd, zeros, zeros)
    y_ctx = readout(o, g, hp.dtype)
    s0 = s0_lat.astype(jnp.float32)
    q, v, fwd, bwd, g = _rec_project(hs, w_in, lb)
    o, _, _ = _bidir_scan(q, v, fwd, bwd, s0[:, 0], s0[:, 1])
    y_lat = readout(o, g, hs.dtype)
    state = jnp.stack([s_f, s_b], axis=1).astype(hp.dtype)
    return y_ctx, y_lat, state


def _attn_project(h, w_qkv, qn_w, kn_w):
    B, L, _ = h.shape
    z = h @ w_qkv
    q, k, v = jnp.split(z, [ATTN_HEADS * HEAD_DIM, (ATTN_HEADS + ATTN_KV_HEADS) * HEAD_DIM], axis=-1)
    q = _rms_norm(q.reshape(B, L, ATTN_HEADS, HEAD_DIM), qn_w)
    k = _rms_norm(k.reshape(B, L, ATTN_KV_HEADS, HEAD_DIM), kn_w)
    return q, k, v.reshape(B, L, ATTN_KV_HEADS, HEAD_DIM)


def _softmax_with_sink(parts, sink):
    lead = parts[0].shape[:-1]
    sink_col = jnp.broadcast_to(sink.astype(jnp.float32).reshape(1, ATTN_KV_HEADS, ATTN_GROUPS, 1, 1), lead + (1,))
    p = jax.nn.softmax(jnp.concatenate(parts + [sink_col], axis=-1), axis=-1)
    out, start = [], 0
    for part in parts:
        out.append(p[..., start:start + part.shape[-1]])
        start += part.shape[-1]
    return out


def _ctx_attention(q, k, v, sink):
    B, L = q.shape[:2]
    qg = q.reshape(B, L, ATTN_KV_HEADS, ATTN_GROUPS, HEAD_DIM)
    s = jnp.einsum('blkgd,bskd->bkgls', qg, k, preferred_element_type=jnp.float32) * ATTN_SCALE
    (p,) = _softmax_with_sink([s], sink)
    o = jnp.einsum('bkgls,bskd->blkgd', p.astype(v.dtype), v)
    return o.reshape(B, L, ATTN_HEADS * HEAD_DIM)


def _latent_attention(q, k, v, k_ctx, v_ctx, sink):
    B, L = q.shape[:2]
    n = L // WINDOW
    qg = q.reshape(B, L, ATTN_KV_HEADS, ATTN_GROUPS, HEAD_DIM)
    pad = ((0, 0), (WINDOW, WINDOW), (0, 0), (0, 0))
    kp = jnp.pad(k, pad)
    vp = jnp.pad(v, pad)
    offs_q = jnp.arange(WINDOW)
    offs_k = jnp.arange(3 * WINDOW) - WINDOW

    def block(j):
        start = j * WINDOW
        qb = lax.dynamic_slice_in_dim(qg, start, WINDOW, axis=1)
        kb = lax.dynamic_slice_in_dim(kp, start, 3 * WINDOW, axis=1)
        vb = lax.dynamic_slice_in_dim(vp, start, 3 * WINDOW, axis=1)
        qpos = start + offs_q
        kpos = start + offs_k
        valid = (kpos[None, :] >= 0) & (kpos[None, :] < L) & (jnp.abs(qpos[:, None] - kpos[None, :]) <= WINDOW)
        s_loc = jnp.einsum('bqkgd,bskd->bkgqs', qb, kb, preferred_element_type=jnp.float32) * ATTN_SCALE
        s_loc = jnp.where(valid, s_loc, NEG_INF)
        s_ctx = jnp.einsum('bqkgd,bskd->bkgqs', qb, k_ctx, preferred_element_type=jnp.float32) * ATTN_SCALE
        p_loc, p_ctx = _softmax_with_sink([s_loc, s_ctx], sink)
        return (jnp.einsum('bkgqs,bskd->bqkgd', p_loc.astype(vb.dtype), vb)
                + jnp.einsum('bkgqs,bskd->bqkgd', p_ctx.astype(v_ctx.dtype), v_ctx))

    o = lax.map(block, jnp.arange(n))
    return jnp.moveaxis(o, 0, 1).reshape(B, L, ATTN_HEADS * HEAD_DIM)


def _attn_mixer(hp, hs, k_cache, v_cache, w_qkv, qn_w, kn_w, sink, w_out):
    qp, kp, vp = _attn_project(hp, w_qkv, qn_w, kn_w)
    y_ctx = _ctx_attention(qp, kp, vp, sink) @ w_out
    qs, ks, vs = _attn_project(hs, w_qkv, qn_w, kn_w)
    qs, ks = _rope_2d(qs), _rope_2d(ks)
    y_lat = _latent_attention(qs, ks, vs, k_cache, v_cache, sink) @ w_out
    return y_ctx, y_lat, kp, vp


def _peer(x, w_q, sub_keys, u, v):
    T = x.shape[0]
    q = (x @ w_q).reshape(T, PEER_HEADS, 2, PEER_DQ_HALF)
    s = jnp.einsum('thpd,hpnd->thpn', q, sub_keys, preferred_element_type=jnp.float32)
    s1, i1 = lax.top_k(s[:, :, 0], PEER_TOPK)
    s2, i2 = lax.top_k(s[:, :, 1], PEER_TOPK)
    cand = (s1[..., :, None] + s2[..., None, :]).reshape(T, PEER_HEADS, PEER_TOPK * PEER_TOPK)
    cidx = (i1[..., :, None] * PEER_NKEYS + i2[..., None, :]).reshape(T, PEER_HEADS, PEER_TOPK * PEER_TOPK)
    best, pos = lax.top_k(cand, PEER_TOPK)
    eidx = jnp.take_along_axis(cidx, pos, axis=-1)
    gate = jax.nn.softmax(best, axis=-1)
    nb = T // PEER_TOKEN_BLOCK
    xs = x.reshape(nb, PEER_TOKEN_BLOCK, D_MODEL)
    ids = eidx.reshape(nb, PEER_TOKEN_BLOCK, PEER_HEADS * PEER_TOPK)
    gs = gate.reshape(nb, PEER_TOKEN_BLOCK, PEER_HEADS * PEER_TOPK).astype(x.dtype)

    def block(args):
        xb, ib, gb = args
        hb = jax.nn.gelu(jnp.einsum('tkd,td->tk', u[ib], xb), approximate=False)
        return jnp.einsum('tk,tkd->td', gb * hb, v[ib])

    return lax.map(block, (xs, ids, gs)).reshape(T, D_MODEL)


def setup_inputs(seed: int = 0) -> dict:
    key = jax.random.key(seed)
    ks = jax.random.split(key, 23)
    D = D_MODEL
    nrm = lambda k, shape, s: jax.random.normal(k, shape, jnp.float32) * s
    return {
        'x_prompt': nrm(ks[0], (BATCH, SEQ, D), 1.0),
        'x_sample': nrm(ks[1], (DEC_BATCH, DEC_SEQ, D), 1.0),
        'c': nrm(ks[2], (DEC_BATCH, D), 1.0),
        'state_rec': nrm(ks[3], (DEC_BATCH, N_REC_LAYERS, 2, REC_HEADS, REC_DK, REC_DV), 0.5),
        'cache_k': nrm(ks[4], (DEC_BATCH, N_ATTN_LAYERS, PAST_LEN, ATTN_KV_HEADS, HEAD_DIM), 1.0),
        'cache_v': nrm(ks[5], (DEC_BATCH, N_ATTN_LAYERS, PAST_LEN, ATTN_KV_HEADS, HEAD_DIM), 1.0),
        'c_ctx': nrm(ks[6], (D,), 1.0),
        'norm_w': 1.0 + nrm(ks[7], (DEPTH, 2, D), 0.02),
        'ada_w': nrm(ks[8], (DEPTH, D, N_MOD * D), 0.5 * D ** -0.5),
        'ada_b': nrm(ks[9], (DEPTH, N_MOD * D), 0.02),
        'rec_w_in': nrm(ks[10], (N_REC_LAYERS, D, 5 * D), D ** -0.5),
        'rec_lb': nrm(ks[11], (N_REC_LAYERS, 2, REC_HEADS * REC_DK), 0.5),
        'rec_gn_w': 1.0 + nrm(ks[12], (N_REC_LAYERS, REC_DV), 0.02),
        'rec_w_out': nrm(ks[13], (N_REC_LAYERS, D, D), D ** -0.5),
        'attn_w_qkv': nrm(ks[14], (N_ATTN_LAYERS, D, (ATTN_HEADS + 2 * ATTN_KV_HEADS) * HEAD_DIM), D ** -0.5),
        'attn_qn_w': 1.0 + nrm(ks[15], (N_ATTN_LAYERS, HEAD_DIM), 0.02),
        'attn_kn_w': 1.0 + nrm(ks[16], (N_ATTN_LAYERS, HEAD_DIM), 0.02),
        'attn_sink': nrm(ks[17], (N_ATTN_LAYERS, ATTN_HEADS), 1.0),
        'attn_w_out': nrm(ks[18], (N_ATTN_LAYERS, ATTN_HEADS * HEAD_DIM, D), (ATTN_HEADS * HEAD_DIM) ** -0.5),
        'peer_w_q': nrm(ks[19], (DEPTH, D, PEER_HEADS * PEER_DQ), D ** -0.5),
        'peer_keys': nrm(ks[20], (DEPTH, PEER_HEADS, 2, PEER_NKEYS, PEER_DQ_HALF), PEER_DQ_HALF ** -0.5),
        'peer_u': nrm(ks[21], (DEPTH, PEER_EXPERTS, D), D ** -0.5),
        'peer_v': nrm(ks[22], (DEPTH, PEER_EXPERTS, D), PEER_HEADS ** -0.5),
    }


def reference(x_prompt, x_sample, c, state_rec, cache_k, cache_v, c_ctx, norm_w, ada_w, ada_b,
              rec_w_in, rec_lb, rec_gn_w, rec_w_out, attn_w_qkv, attn_qn_w, attn_kn_w, attn_sink,
              attn_w_out, peer_w_q, peer_keys, peer_u, peer_v):
    lb_p = jax.nn.softmax(rec_lb.astype(jnp.float32), axis=0)
    lb_cs = jnp.cumsum(lb_p, axis=0)
    lower_bounds = lb_cs - lb_cs[0]

    xp, xs = x_prompt, x_sample
    Bp, Lp = xp.shape[:2]
    Bs, Ls = xs.shape[:2]
    cond_ctx = jax.nn.silu(c_ctx)[None, :]
    cond_lat = jax.nn.silu(c)
    new_rec, new_k, new_v = [], [], []
    for i in range(DEPTH):
        mod_p = (cond_ctx @ ada_w[i] + ada_b[i]).reshape(1, N_MOD, 1, D_MODEL)
        mod_s = (cond_lat @ ada_w[i] + ada_b[i]).reshape(Bs, N_MOD, 1, D_MODEL)
        hp = _modulate(_rms_norm(xp, norm_w[i, 0]), mod_p[:, 0], mod_p[:, 1])
        hs = _modulate(_rms_norm(xs, norm_w[i, 0]), mod_s[:, 0], mod_s[:, 1])
        r = i // N_MIXERS
        if i % N_MIXERS == 0:
            yp, ys, st = _rec_mixer(hp, hs, state_rec[:, r], rec_w_in[r], lower_bounds[r],
                                    rec_gn_w[r], rec_w_out[r])
            new_rec.append(st)
        else:
            yp, ys, kc, vc = _attn_mixer(hp, hs, cache_k[:, r], cache_v[:, r], attn_w_qkv[r],
                                         attn_qn_w[r], attn_kn_w[r], attn_sink[r], attn_w_out[r])
            new_k.append(kc)
            new_v.append(vc)
        xp = xp + mod_p[:, 2] * yp
        xs = xs + mod_s[:, 2] * ys
        hp = _modulate(_rms_norm(xp, norm_w[i, 1]), mod_p[:, 3], mod_p[:, 4])
        hs = _modulate(_rms_norm(xs, norm_w[i, 1]), mod_s[:, 3], mod_s[:, 4])
        tokens = jnp.concatenate([hp.reshape(-1, D_MODEL), hs.reshape(-1, D_MODEL)], axis=0)
        f = _peer(tokens, peer_w_q[i], peer_keys[i], peer_u[i], peer_v[i])
        xp = xp + mod_p[:, 5] * f[:Bp * Lp].reshape(Bp, Lp, D_MODEL)
        xs = xs + mod_s[:, 5] * f[Bp * Lp:].reshape(Bs, Ls, D_MODEL)
    new_state_rec = jnp.stack(new_rec, axis=1)
    new_cache_k = jnp.stack(new_k, axis=1)
    new_cache_v = jnp.stack(new_v, axis=1)
    return (xp, xs, new_state_rec, new_cache_k, new_cache_v)
```

```python
import functools

import numpy as np
import jax
import jax.numpy as jnp
from jax import lax
from jax.experimental import pallas as pl
from jax.experimental.pallas import tpu as pltpu

F32 = jnp.float32
BF16 = jnp.bfloat16

D_MODEL = 1024
BATCH = 16
SEQ = 256
DEPTH = 4
DEC_BATCH = 8
DEC_SEQ = 2048
PAST_LEN = 512
GRID_W = 64
N_MOD = 6
EPS = 1e-6
NEG_INF = -1e30
REC_HEADS = 8
REC_DK = 128
REC_CHUNK = 64
ATTN_HEADS = 16
ATTN_KV_HEADS = 4
ATTN_GROUPS = 4
HEAD_DIM = 64
WINDOW = 128
ATTN_SCALE = HEAD_DIM ** -0.5
ROPE_BASE = 10000.0
PEER_HEADS = 8
PEER_NKEYS = 128
PEER_EXPERTS = PEER_NKEYS * PEER_NKEYS
PEER_TOPK = 16
PEER_DQ = 256

CTX_TOKENS = BATCH * SEQ
LAT_TOKENS = DEC_BATCH * DEC_SEQ
TOKENS = CTX_TOKENS + LAT_TOKENS
N_GROUPS = 16

V7X_VMEM_LIMIT = 48 * 1024 * 1024
LANES = 128
TM = 512
NOT_RANKED = 99.0


def _cparams(*sem):
    return pltpu.CompilerParams(dimension_semantics=sem, vmem_limit_bytes=V7X_VMEM_LIMIT)


def _group_of_tile(i, tm):
    ctx_tiles = CTX_TOKENS // tm
    per_seq = DEC_SEQ // tm
    return jnp.where(i < ctx_tiles, 0, 1 + jnp.maximum(i - ctx_tiles, 0) // per_seq)


def _split2(a):
    hi = a.astype(BF16)
    lo = (a - hi.astype(F32)).astype(BF16)
    return hi, lo


def _split3(a):
    p1 = a.astype(BF16)
    r1 = a - p1.astype(F32)
    p2 = r1.astype(BF16)
    p3 = (r1 - p2.astype(F32)).astype(BF16)
    return p1, p2, p3


def _dot(a, b):
    return jnp.dot(a, b, preferred_element_type=F32)


def _dot_nt(a, b):
    return lax.dot_general(a, b, (((1,), (1,)), ((), ())), preferred_element_type=F32)


def _dot_tn(a, b):
    return lax.dot_general(a, b, (((0,), (0,)), ((), ())), preferred_element_type=F32)


def _exact_left(sel, a):
    p1, p2, p3 = _split3(a)
    return _dot(sel, p1) + _dot(sel, p2) + _dot(sel, p3)


def _exact_right(a, sel):
    p1, p2, p3 = _split3(a)
    return _dot(p1, sel) + _dot(p2, sel) + _dot(p3, sel)


def _silu(x):
    return x * jax.nn.sigmoid(x)


def _norm_mod(x, nw, shift, scale):
    ms = jnp.mean(x * x, axis=-1, keepdims=True)
    y = x * lax.rsqrt(ms + EPS) * nw
    return y * (1.0 + scale) + shift


def _ada_kernel(cond_ref, w_ref, b_ref, o_ref):
    a = _silu(cond_ref[...])
    a_hi, a_lo = _split2(a)
    w_hi, w_lo = _split2(w_ref[...])
    o_ref[...] = _dot(a_hi, w_hi) + _dot(a_hi, w_lo) + _dot(a_lo, w_hi) + b_ref[...]


def _ada_mods(cond, ada_w, ada_b):
    tn = 1024
    n = N_MOD * D_MODEL
    return pl.pallas_call(
        _ada_kernel,
        grid=(DEPTH, n // tn),
        in_specs=[
            pl.BlockSpec((N_GROUPS, D_MODEL), lambda l, j: (0, 0)),
            pl.BlockSpec((None, D_MODEL, tn), lambda l, j: (l, 0, j)),
            pl.BlockSpec((None, 1, tn), lambda l, j: (l, 0, j)),
        ],
        out_specs=pl.BlockSpec((None, N_GROUPS, tn), lambda l, j: (l, 0, j)),
        out_shape=jax.ShapeDtypeStruct((DEPTH, N_GROUPS, n), F32),
        compiler_params=_cparams("parallel", "parallel"),
        name="ada_mods",
    )(cond, ada_w, ada_b.reshape(DEPTH, 1, n))


def _nm_kernel(x_ref, nw_ref, mod_ref, w_ref, o_ref, h_scr, *, shift, scale):
    @pl.when(pl.program_id(1) == 0)
    def _():
        h = _norm_mod(x_ref[...], nw_ref[...], mod_ref[shift:shift + 1, :], mod_ref[scale:scale + 1, :])
        h_scr[...] = h.astype(BF16)

    o_ref[...] = _dot(h_scr[...], w_ref[...])


def _norm_mod_matmul(x, nw, mods, w_bf16, shift, scale, tn):
    t, d = x.shape
    n = w_bf16.shape[1]
    return pl.pallas_call(
        functools.partial(_nm_kernel, shift=shift, scale=scale),
        grid=(t // TM, n // tn),
        in_specs=[
            pl.BlockSpec((TM, d), lambda i, j: (i, 0)),
            pl.BlockSpec((1, d), lambda i, j: (0, 0)),
            pl.BlockSpec((None, N_MOD, d), lambda i, j: (_group_of_tile(i, TM), 0, 0)),
            pl.BlockSpec((d, tn), lambda i, j: (0, j)),
        ],
        out_specs=pl.BlockSpec((TM, tn), lambda i, j: (i, j)),
        out_shape=jax.ShapeDtypeStruct((t, n), F32),
        scratch_shapes=[pltpu.VMEM((TM, d), BF16)],
        compiler_params=_cparams("parallel", "arbitrary"),
        name="norm_mod_matmul",
    )(x, nw.reshape(1, d), mods, w_bf16)


def _nm_query_kernel(x_ref, nw_ref, mod_ref, whi_ref, wlo_ref, q_ref, hb_ref, hhi_scr, hlo_scr, *, shift, scale):
    @pl.when(pl.program_id(1) == 0)
    def _():
        h = _norm_mod(x_ref[...], nw_ref[...], mod_ref[shift:shift + 1, :], mod_ref[scale:scale + 1, :])
        hi, lo = _split2(h)
        hhi_scr[...] = hi
        hlo_scr[...] = lo
        hb_ref[...] = hi

    hi = hhi_scr[...]
    q_ref[...] = _dot(hi, whi_ref[...]) + _dot(hi, wlo_ref[...]) + _dot(hlo_scr[...], whi_ref[...])


def _norm_mod_query(x, nw, mods, w_hi, w_lo, shift, scale, tn):
    t, d = x.shape
    n = w_hi.shape[1]
    return pl.pallas_call(
        functools.partial(_nm_query_kernel, shift=shift, scale=scale),
        grid=(t // TM, n // tn),
        in_specs=[
            pl.BlockSpec((TM, d), lambda i, j: (i, 0)),
            pl.BlockSpec((1, d), lambda i, j: (0, 0)),
            pl.BlockSpec((None, N_MOD, d), lambda i, j: (_group_of_tile(i, TM), 0, 0)),
            pl.BlockSpec((d, tn), lambda i, j: (0, j)),
            pl.BlockSpec((d, tn), lambda i, j: (0, j)),
        ],
        out_specs=[
            pl.BlockSpec((TM, tn), lambda i, j: (i, j)),
            pl.BlockSpec((TM, d), lambda i, j: (i, 0)),
        ],
        out_shape=[jax.ShapeDtypeStruct((t, n), F32), jax.ShapeDtypeStruct((t, d), BF16)],
        scratch_shapes=[pltpu.VMEM((TM, d), BF16), pltpu.VMEM((TM, d), BF16)],
        compiler_params=_cparams("parallel", "arbitrary"),
        name="norm_mod_query",
    )(x, nw.reshape(1, d), mods, w_hi, w_lo)


REC_LEVELS = (32, 16, 8, 4, 2, 1)
REC_ROWS_B = 64 * len(REC_LEVELS)
REC_ROWS_REM = REC_ROWS_B + 64
REC_ROWS_TOT = REC_ROWS_REM + 64
REC_ROWS = REC_ROWS_TOT + 8
REC_STEP = 256


def _rec_consts():
    c = REC_CHUNK
    dm = np.zeros((2, REC_ROWS, c), np.float32)
    mk = np.zeros((2, len(REC_LEVELS) + 1, c, c), np.float32)
    for lvl, half in enumerate(REC_LEVELS):
        blk = 2 * half
        for r in range(c):
            t0 = (r // blk) * blk + half
            if r % blk >= half:
                dm[0, lvl * c + r, t0 + 1:r + 1] = 1.0
            else:
                dm[0, lvl * c + r, r + 1:t0 + 1] = 1.0
        for t in range(c):
            for s in range(c):
                if t // blk == s // blk and t % blk >= half and s % blk < half:
                    mk[0, lvl, t, s] = 1.0
    mk[0, len(REC_LEVELS)] = np.eye(c)
    for t in range(c):
        dm[0, REC_ROWS_B + t, :t + 1] = 1.0
        dm[0, REC_ROWS_REM + t, t + 1:] = 1.0
    dm[0, REC_ROWS_TOT:, :] = 1.0
    for b0 in range(0, REC_ROWS_TOT, c):
        dm[1, b0:b0 + c] = dm[0, b0:b0 + c][::-1, ::-1]
    dm[1, REC_ROWS_TOT:] = 1.0
    mk[1] = mk[0][:, ::-1, ::-1]
    return dm, mk


def _rec_scan_kernel(*refs, zero_init, emit_state):
    zq_ref, zf_ref, zi_ref, lbp_ref, dm_ref, mk_ref = refs[:6]
    pos = 6
    if not zero_init:
        s0_ref = refs[pos]
        pos += 1
    o_ref = refs[pos]
    pos += 1
    if emit_state:
        sout_ref = refs[pos]
        pos += 1
    st_scr = refs[pos]

    direction = pl.program_id(0)
    step = pl.program_id(2)
    nchunk = REC_STEP // REC_CHUNK

    @pl.when(step == 0)
    def _():
        for h in range(REC_HEADS):
            if zero_init:
                st_scr[h] = jnp.zeros((REC_DK, REC_DK), F32)
            else:
                st_scr[h] = s0_ref[h].T

    dmat = dm_ref[...]

    def chunk_body(ci, carry):
        cc = jnp.where(direction == 0, ci, nchunk - 1 - ci)
        rows = pl.ds(pl.multiple_of(cc * REC_CHUNK, REC_CHUNK), REC_CHUNK)
        for h in range(REC_HEADS):
            cols = slice(h * REC_DK, (h + 1) * REC_DK)
            q = _silu(zq_ref[rows, cols])
            fz = zf_ref[rows, cols]
            v = zi_ref[rows, cols].astype(BF16)
            log_lb = lbp_ref[0:1, cols]
            log_1m = lbp_ref[1:2, cols]
            one_m = lbp_ref[2:3, cols]
            log_sig = jnp.minimum(fz, 0.0) - jnp.log1p(jnp.exp(-jnp.abs(fz)))
            cterm = log_1m + log_sig
            logf = jnp.maximum(log_lb, cterm) + jnp.log1p(jnp.exp(-jnp.abs(log_lb - cterm)))
            k = one_m * jax.nn.sigmoid(-fz)

            ex = _exact_left(dmat, logf)
            sc = mk_ref[len(REC_LEVELS)] * _dot_nt(q.astype(BF16), k.astype(BF16))
            for lvl in range(len(REC_LEVELS)):
                e = jnp.exp(ex[lvl * REC_CHUNK:(lvl + 1) * REC_CHUNK])
                sc = sc + mk_ref[lvl] * _dot_nt((q * e).astype(BF16), (k * e).astype(BF16))
            st = st_scr[h]
            qb = (q * jnp.exp(ex[REC_ROWS_B:REC_ROWS_B + REC_CHUNK])).astype(BF16)
            o = _dot(sc.astype(BF16), v) + _dot_nt(qb, st.astype(BF16))
            o_ref[rows, cols] = o
            kd = (k * jnp.exp(ex[REC_ROWS_REM:REC_ROWS_REM + REC_CHUNK])).astype(BF16)
            tot = jnp.exp(ex[REC_ROWS_TOT:REC_ROWS_TOT + 1])
            st_scr[h] = st * tot + _dot_tn(v, kd)
        return carry

    lax.fori_loop(0, nchunk, chunk_body, 0)

    if emit_state:
        @pl.when(step == pl.num_programs(2) - 1)
        def _():
            for h in range(REC_HEADS):
                sout_ref[h] = st_scr[h].T


def _rec_scan(z, lbp, dm, mk, row0, nseq, seqlen, s0, layer_r, emit_state):
    steps = seqlen // REC_STEP
    blk0 = row0 // REC_STEP
    d = D_MODEL

    def row_blk(dr, b, c):
        return blk0 + b * steps + jnp.where(dr == 0, c, steps - 1 - c)

    def out_blk(dr, b, c):
        return b * steps + jnp.where(dr == 0, c, steps - 1 - c)

    in_specs = [
        pl.BlockSpec((REC_STEP, d), lambda dr, b, c: (row_blk(dr, b, c), 0)),
        pl.BlockSpec((REC_STEP, d), lambda dr, b, c: (row_blk(dr, b, c), 1 + dr)),
        pl.BlockSpec((REC_STEP, d), lambda dr, b, c: (row_blk(dr, b, c), 3)),
        pl.BlockSpec((None, 8, d), lambda dr, b, c: (dr, 0, 0)),
        pl.BlockSpec((None, REC_ROWS, REC_CHUNK), lambda dr, b, c: (dr, 0, 0)),
        pl.BlockSpec((None, len(REC_LEVELS) + 1, REC_CHUNK, REC_CHUNK), lambda dr, b, c: (dr, 0, 0, 0)),
    ]
    args = [z, z, z, lbp, dm, mk]
    if s0 is not None:
        in_specs.append(pl.BlockSpec((None, None, None, REC_HEADS, REC_DK, REC_DK),
                                     lambda dr, b, c: (b, layer_r, dr, 0, 0, 0)))
        args.append(s0)
    out_specs = [pl.BlockSpec((None, REC_STEP, d), lambda dr, b, c: (dr, out_blk(dr, b, c), 0))]
    out_shape = [jax.ShapeDtypeStruct((2, nseq * seqlen, d), F32)]
    if emit_state:
        out_specs.append(pl.BlockSpec((None, None, REC_HEADS, REC_DK, REC_DK), lambda dr, b, c: (b, dr, 0, 0, 0)))
        out_shape.append(jax.ShapeDtypeStruct((nseq, 2, REC_HEADS, REC_DK, REC_DK), F32))
    return pl.pallas_call(
        functools.partial(_rec_scan_kernel, zero_init=s0 is None, emit_state=emit_state),
        grid=(2, nseq, steps),
        in_specs=in_specs,
        out_specs=out_specs,
        out_shape=out_shape,
        scratch_shapes=[pltpu.VMEM((REC_HEADS, REC_DK, REC_DK), F32)],
        compiler_params=_cparams("parallel", "parallel", "arbitrary"),
        name="rec_scan",
    )(*args)


def _rec_out_kernel(o_ref, zg_ref, x_ref, mod_ref, gn_ref, w_ref, out_ref, y_scr, *, gate):
    for h in range(REC_HEADS):
        cols = slice(h * REC_DK, (h + 1) * REC_DK)
        o = o_ref[0, :, cols] + o_ref[1, :, cols]
        ms = jnp.mean(o * o, axis=-1, keepdims=True)
        y = o * lax.rsqrt(ms + EPS) * gn_ref[:, cols] * _silu(zg_ref[:, cols])
        y_scr[:, cols] = y.astype(BF16)
    out_ref[...] = x_ref[...] + mod_ref[gate:gate + 1, :] * _dot(y_scr[...], w_ref[...])


def _rec_out(o, z, x, mods, gn_tiled, w_out_bf16, gate):
    t, d = x.shape
    return pl.pallas_call(
        functools.partial(_rec_out_kernel, gate=gate),
        grid=(t // TM,),
        in_specs=[
            pl.BlockSpec((2, TM, d), lambda i: (0, i, 0)),
            pl.BlockSpec((TM, d), lambda i: (i, 4)),
            pl.BlockSpec((TM, d), lambda i: (i, 0)),
            pl.BlockSpec((None, N_MOD, d), lambda i: (_group_of_tile(i, TM), 0, 0)),
            pl.BlockSpec((1, d), lambda i: (0, 0)),
            pl.BlockSpec((d, d), lambda i: (0, 0)),
        ],
        out_specs=pl.BlockSpec((TM, d), lambda i: (i, 0)),
        out_shape=jax.ShapeDtypeStruct((t, d), F32),
        scratch_shapes=[pltpu.VMEM((TM, d), BF16)],
        compiler_params=_cparams("parallel"),
        name="rec_out",
    )(o, z, x, mods, gn_tiled, w_out_bf16)


QK_COLS = (ATTN_HEADS + ATTN_KV_HEADS) * HEAD_DIM
Q_COLS = ATTN_HEADS * HEAD_DIM
KV_COLS = ATTN_KV_HEADS * HEAD_DIM
ROPE_SHIFT = HEAD_DIM // 4


def _attn_prep_kernel(z_ref, wt_ref, gm_ref, gmt_ref, cos_ref, sa_ref, sb_ref, q_ref, k_ref):
    x = z_ref[...]
    ss = _exact_right(x * x, gm_ref[...])
    r = lax.rsqrt(ss * (1.0 / HEAD_DIM) + EPS)
    y = x * _exact_right(r, gmt_ref[...]) * wt_ref[...]
    is_lat = pl.program_id(0) >= CTX_TOKENS // TM

    @pl.when(is_lat)
    def _():
        cos, sa, sb = cos_ref[...], sa_ref[...], sb_ref[...]
        for c in range(QK_COLS // LANES):
            yc = y[:, c * LANES:(c + 1) * LANES]
            rc = (yc * cos + pltpu.roll(yc, LANES - ROPE_SHIFT, axis=1) * sa
                  + pltpu.roll(yc, ROPE_SHIFT, axis=1) * sb)
            if c < Q_COLS // LANES:
                q_ref[:, c * LANES:(c + 1) * LANES] = rc.astype(BF16)
            else:
                k_ref[:, c * LANES - Q_COLS:(c + 1) * LANES - Q_COLS] = rc

    @pl.when(jnp.logical_not(is_lat))
    def _():
        q_ref[...] = y[:, :Q_COLS].astype(BF16)
        k_ref[...] = y[:, Q_COLS:]


def _rope_tables():
    half = HEAD_DIM // 2
    nf = half // 2
    pos = jnp.arange(DEC_SEQ)
    row = (pos // GRID_W).astype(F32)
    col = (pos % GRID_W).astype(F32)
    freqs = ROPE_BASE ** (-jnp.arange(nf, dtype=F32) / nf)
    ang_r = row[:, None] * freqs[None, :]
    ang_c = col[:, None] * freqs[None, :]
    zeros = jnp.zeros_like(ang_r)
    cos = jnp.concatenate([jnp.cos(ang_r), jnp.cos(ang_r), jnp.cos(ang_c), jnp.cos(ang_c)], axis=-1)
    sa = jnp.concatenate([-jnp.sin(ang_r), zeros, -jnp.sin(ang_c), zeros], axis=-1)
    sb = jnp.concatenate([zeros, jnp.sin(ang_r), zeros, jnp.sin(ang_c)], axis=-1)
    tile = lambda a: jnp.tile(a, (1, LANES // HEAD_DIM))
    return tile(cos), tile(sa), tile(sb)


def _attn_prep(z, wt, rope):
    t = z.shape[0]
    gm = np.zeros((QK_COLS, LANES), np.float32)
    gm[np.arange(QK_COLS), np.arange(QK_COLS) // HEAD_DIM] = 1.0
    ctx_tiles = CTX_TOKENS // TM
    per_seq = DEC_SEQ // TM
    pos_blk = lambda i: (jnp.maximum(i - ctx_tiles, 0) % per_seq, 0)
    return pl.pallas_call(
        _attn_prep_kernel,
        grid=(t // TM,),
        in_specs=[
            pl.BlockSpec((TM, QK_COLS), lambda i: (i, 0)),
            pl.BlockSpec((1, QK_COLS), lambda i: (0, 0)),
            pl.BlockSpec((QK_COLS, LANES), lambda i: (0, 0)),
            pl.BlockSpec((LANES, QK_COLS), lambda i: (0, 0)),
            pl.BlockSpec((TM, LANES), pos_blk),
            pl.BlockSpec((TM, LANES), pos_blk),
            pl.BlockSpec((TM, LANES), pos_blk),
        ],
        out_specs=[
            pl.BlockSpec((TM, Q_COLS), lambda i: (i, 0)),
            pl.BlockSpec((TM, KV_COLS), lambda i: (i, 0)),
        ],
        out_shape=[jax.ShapeDtypeStruct((t, Q_COLS), BF16), jax.ShapeDtypeStruct((t, KV_COLS), F32)],
        compiler_params=_cparams("parallel"),
        name="attn_prep",
    )(z, wt, jnp.asarray(gm, BF16), jnp.asarray(gm.T, BF16), *rope)


def _attn_kernel(*refs, local, tq):
    sink_ref, q_ref = refs[0], refs[1]
    if local:
        kl_refs = refs[2:5]
        vl_refs = refs[5:8]
        kc_ref, vc_ref, o_ref = refs[8:11]
        j = pl.program_id(1)
        nblk = pl.num_programs(1)
    else:
        kc_ref, vc_ref, o_ref = refs[2:5]
    rows = ATTN_GROUPS * tq
    row_id = lax.broadcasted_iota(jnp.int32, (rows, 1), 0)
    if local:
        t_id = lax.broadcasted_iota(jnp.int32, (rows, 3 * WINDOW), 0) % tq
        c_id = lax.broadcasted_iota(jnp.int32, (rows, 3 * WINDOW), 1)
        rel = c_id - WINDOW - t_id
        kpos = (j - 1) * WINDOW + c_id
        valid = (jnp.abs(rel) <= WINDOW) & (kpos >= 0) & (kpos < nblk * WINDOW)
    outs = []
    for h in range(ATTN_KV_HEADS):
        cols = slice(h * HEAD_DIM, (h + 1) * HEAD_DIM)
        qs = jnp.concatenate(
            [q_ref[:, (h * ATTN_GROUPS + g) * HEAD_DIM:(h * ATTN_GROUPS + g + 1) * HEAD_DIM]
             for g in range(ATTN_GROUPS)], axis=0)
        sink = jnp.zeros((rows, 1), F32)
        for g in range(ATTN_GROUPS):
            sink = jnp.where(row_id // tq == g, sink_ref[h * ATTN_GROUPS + g], sink)
        s_ctx = _dot_nt(qs, kc_ref[:, cols].astype(BF16)) * ATTN_SCALE
        m = jnp.maximum(jnp.max(s_ctx, axis=-1, keepdims=True), sink)
        if local:
            kl = jnp.concatenate([r[:, cols] for r in kl_refs], axis=0).astype(BF16)
            vl = jnp.concatenate([r[:, cols] for r in vl_refs], axis=0).astype(BF16)
            s_loc = jnp.where(valid, _dot_nt(qs, kl) * ATTN_SCALE, NEG_INF)
            m = jnp.maximum(m, jnp.max(s_loc, axis=-1, keepdims=True))
        p_ctx = jnp.exp(s_ctx - m)
        den = jnp.sum(p_ctx, axis=-1, keepdims=True) + jnp.exp(sink - m)
        acc = _dot(p_ctx.astype(BF16), vc_ref[:, cols].astype(BF16))
        if local:
            p_loc = jnp.exp(s_loc - m)
            den = den + jnp.sum(p_loc, axis=-1, keepdims=True)
            acc = acc + _dot(p_loc.astype(BF16), vl)
        o = acc / den
        outs.extend(o[g * tq:(g + 1) * tq] for g in range(ATTN_GROUPS))
    o_ref[...] = jnp.concatenate(outs, axis=-1).astype(BF16)


def _attn_ctx(sink, qn, kn, z):
    vblk = Q_COLS // KV_COLS + 1
    return pl.pallas_call(
        functools.partial(_attn_kernel, local=False, tq=SEQ),
        grid=(BATCH,),
        in_specs=[
            pl.BlockSpec(memory_space=pltpu.SMEM),
            pl.BlockSpec((SEQ, Q_COLS), lambda b: (b, 0)),
            pl.BlockSpec((SEQ, KV_COLS), lambda b: (b, 0)),
            pl.BlockSpec((SEQ, KV_COLS), lambda b: (b, vblk)),
        ],
        out_specs=pl.BlockSpec((SEQ, Q_COLS), lambda b: (b, 0)),
        out_shape=jax.ShapeDtypeStruct((CTX_TOKENS, Q_COLS), BF16),
        compiler_params=_cparams("parallel"),
        name="attn_ctx",
    )(sink, qn, kn, z)


def _attn_lat(sink, qn, kn, z, cache_k, cache_v, layer_r):
    nblk = DEC_SEQ // WINDOW
    blk0 = CTX_TOKENS // WINDOW
    vblk = Q_COLS // KV_COLS + 1

    def nbr(off):
        return lambda b, j: (blk0 + b * nblk + jnp.clip(j + off, 0, nblk - 1), 0)

    def nbr_v(off):
        return lambda b, j: (blk0 + b * nblk + jnp.clip(j + off, 0, nblk - 1), vblk)

    return pl.pallas_call(
        functools.partial(_attn_kernel, local=True, tq=WINDOW),
        grid=(DEC_BATCH, nblk),
        in_specs=[
            pl.BlockSpec(memory_space=pltpu.SMEM),
            pl.BlockSpec((WINDOW, Q_COLS), lambda b, j: (blk0 + b * nblk + j, 0)),
            pl.BlockSpec((WINDOW, KV_COLS), nbr(-1)),
            pl.BlockSpec((WINDOW, KV_COLS), nbr(0)),
            pl.BlockSpec((WINDOW, KV_COLS), nbr(1)),
            pl.BlockSpec((WINDOW, KV_COLS), nbr_v(-1)),
            pl.BlockSpec((WINDOW, KV_COLS), nbr_v(0)),
            pl.BlockSpec((WINDOW, KV_COLS), nbr_v(1)),
            pl.BlockSpec((None, None, PAST_LEN, KV_COLS), lambda b, j: (b, layer_r, 0, 0)),
            pl.BlockSpec((None, None, PAST_LEN, KV_COLS), lambda b, j: (b, layer_r, 0, 0)),
        ],
        out_specs=pl.BlockSpec((WINDOW, Q_COLS), lambda b, j: (b * nblk + j, 0)),
        out_shape=jax.ShapeDtypeStruct((LAT_TOKENS, Q_COLS), BF16),
        compiler_params=_cparams("parallel", "parallel"),
        name="attn_lat",
    )(sink, qn, kn, kn, kn, z, z, z, cache_k, cache_v)


def _proj_res_kernel(a_ref, x_ref, mod_ref, w_ref, out_ref, *, gate):
    out_ref[...] = x_ref[...] + mod_ref[gate:gate + 1, :] * _dot(a_ref[...], w_ref[...])


def _proj_residual(a_ctx, a_lat, x, mods, w_bf16, gate):
    t, d = x.shape
    ctx_tiles = CTX_TOKENS // TM

    def kernel(ac_ref, al_ref, x_ref, mod_ref, w_ref, out_ref):
        i = pl.program_id(0)

        @pl.when(i < ctx_tiles)
        def _():
            _proj_res_kernel(ac_ref, x_ref, mod_ref, w_ref, out_ref, gate=gate)

        @pl.when(i >= ctx_tiles)
        def _():
            _proj_res_kernel(al_ref, x_ref, mod_ref, w_ref, out_ref, gate=gate)

    return pl.pallas_call(
        kernel,
        grid=(t // TM,),
        in_specs=[
            pl.BlockSpec((TM, d), lambda i: (jnp.minimum(i, ctx_tiles - 1), 0)),
            pl.BlockSpec((TM, d), lambda i: (jnp.maximum(i - ctx_tiles, 0), 0)),
            pl.BlockSpec((TM, d), lambda i: (i, 0)),
            pl.BlockSpec((None, N_MOD, d), lambda i: (_group_of_tile(i, TM), 0, 0)),
            pl.BlockSpec((d, d), lambda i: (0, 0)),
        ],
        out_specs=pl.BlockSpec((TM, d), lambda i: (i, 0)),
        out_shape=jax.ShapeDtypeStruct((t, d), F32),
        compiler_params=_cparams("parallel"),
        name="proj_residual",
    )(a_ctx, a_lat, x, mods, w_bf16)


ROUTE_TM = 256
CAND_ROWS = 16 + 8 + 6 * 8 + 8


def _extract_topk(work, order, rounds):
    rank = jnp.full(work.shape, NOT_RANKED, F32)
    vals, ords = [], []
    big = jnp.float32(1e9)
    for r in range(rounds):
        m = jnp.max(work, axis=0, keepdims=True)
        o = jnp.min(jnp.where(work == m, order, big), axis=0, keepdims=True)
        sel = order == o
        rank = jnp.where(sel, jnp.float32(r), rank)
        work = jnp.where(sel, -jnp.inf, work)
        vals.append(m)
        ords.append(o)
    return vals, ords, rank


def _route_kernel(q_ref, khi_ref, klo_ref, r2_ref, b_ref, n_ref, a_ref):
    k = PEER_TOPK
    tm = q_ref.shape[0]
    key_id = lax.broadcasted_iota(jnp.int32, (PEER_NKEYS, tm), 0).astype(F32)
    sub = lax.broadcasted_iota(jnp.int32, (8, tm), 0).astype(F32)
    sub16 = lax.broadcasted_iota(jnp.int32, (k, tm), 0).astype(F32)
    for h in range(PEER_HEADS):
        scores = []
        for p in range(2):
            qs = q_ref[:, (2 * h + p) * LANES:(2 * h + p + 1) * LANES]
            q_hi, q_lo = _split2(qs)
            k_hi, k_lo = khi_ref[h, p], klo_ref[h, p]
            scores.append(_dot_nt(k_hi, q_hi) + _dot_nt(k_hi, q_lo) + _dot_nt(k_lo, q_hi))
        v1, _, rank1 = _extract_topk(scores[0], key_id, k)
        v2, _, rank2 = _extract_topk(scores[1], key_id, k)
        s2s = jnp.concatenate(v2, axis=0)
        s1s_hi = jnp.concatenate(v1[8:], axis=0)
        cand = [v1[0] + s2s, v1[1] + s2s[:8]]
        order = [sub16, k + sub]
        for j1 in range(2, 8):
            cnt = k // (j1 + 1)
            cand.append(jnp.where(sub < cnt, v1[j1] + s2s[:8], -jnp.inf))
            order.append(j1 * k + sub)
        cand.append(s1s_hi + v2[0])
        order.append((8 + sub) * k)
        cand = jnp.concatenate(cand, axis=0)
        order = jnp.concatenate(order, axis=0)
        best, best_ord, _ = _extract_topk(cand, order, k)
        z = jnp.zeros((1, tm), F32)
        nsel = jnp.zeros((PEER_NKEYS, tm), F32)
        for r in range(k):
            z = z + jnp.exp(best[r] - best[0])
            j1 = jnp.floor(best_ord[r] * (1.0 / k))
            nsel = nsel + jnp.where(rank1 == j1, 1.0, 0.0)
        a = jnp.exp(scores[0] - v1[0]) / z
        b = jnp.exp(scores[1] - v2[0])
        r2_ref[h] = rank2.astype(BF16).reshape(PEER_NKEYS // 16, 16, tm)
        b_ref[h] = b.astype(BF16).reshape(PEER_NKEYS // 16, 16, tm)
        n_ref[h] = nsel
        a_ref[h] = a


def _peer_route(q, k_hi, k_lo):
    t = q.shape[0]
    nk = PEER_NKEYS
    slab = pl.BlockSpec((PEER_HEADS, nk // 16, 16, ROUTE_TM), lambda i: (0, 0, 0, i))
    flat = pl.BlockSpec((PEER_HEADS, nk, ROUTE_TM), lambda i: (0, 0, i))
    return pl.pallas_call(
        _route_kernel,
        grid=(t // ROUTE_TM,),
        in_specs=[
            pl.BlockSpec((ROUTE_TM, PEER_HEADS * PEER_DQ), lambda i: (i, 0)),
            pl.BlockSpec((PEER_HEADS, 2, nk, LANES), lambda i: (0, 0, 0, 0)),
            pl.BlockSpec((PEER_HEADS, 2, nk, LANES), lambda i: (0, 0, 0, 0)),
        ],
        out_specs=[slab, slab, flat, flat],
        out_shape=[
            jax.ShapeDtypeStruct((PEER_HEADS, nk // 16, 16, t), BF16),
            jax.ShapeDtypeStruct((PEER_HEADS, nk // 16, 16, t), BF16),
            jax.ShapeDtypeStruct((PEER_HEADS, nk, t), F32),
            jax.ShapeDtypeStruct((PEER_HEADS, nk, t), F32),
        ],
        compiler_params=_cparams("parallel"),
        name="peer_route",
    )(q, k_hi, k_lo)


PEER_TM = 512
PEER_ROWS = 4
INV_SQRT2 = 0.7071067811865476


def _peer_kernel(hb_ref, u_ref, vt_ref, r2_ref, b_ref, n_ref, a_ref, x_ref, mod_ref, out_ref, acc_ref, *, gate):
    j = pl.program_id(1)

    @pl.when(j == 0)
    def _():
        acc_ref[...] = jnp.zeros_like(acc_ref)

    ht = _dot_nt(u_ref[...], hb_ref[...])
    parts = []
    for i in range(PEER_ROWS):
        row = j * PEER_ROWS + i
        g = jnp.zeros((PEER_NKEYS // 16, 16, PEER_TM), BF16)
        for h in range(PEER_HEADS):
            cnt = jnp.broadcast_to(n_ref[h, pl.ds(row, 1), :], (16, PEER_TM)).astype(BF16)
            wgt = jnp.broadcast_to(a_ref[h, pl.ds(row, 1), :], (16, PEER_TM)).astype(BF16)
            g = g + jnp.where(r2_ref[h] < cnt[None], b_ref[h] * wgt[None], jnp.zeros((), BF16))
        hh = ht[i * PEER_NKEYS:(i + 1) * PEER_NKEYS]
        act = 0.5 * hh * (1.0 + lax.erf(hh * INV_SQRT2))
        parts.append(g.reshape(PEER_NKEYS, PEER_TM) * act.astype(BF16))
    at = jnp.concatenate(parts, axis=0)
    acc_ref[...] += _dot(vt_ref[...], at)

    @pl.when(j == pl.num_programs(1) - 1)
    def _():
        out_ref[...] = x_ref[...] + mod_ref[gate:gate + 1, :] * acc_ref[...].T


def _peer_dense(hb, u_bf16, vt_bf16, r2, bw, nsel, aw, x, mods, gate):
    t, d = x.shape
    te = PEER_ROWS * PEER_NKEYS
    nk = PEER_NKEYS
    slab = pl.BlockSpec((PEER_HEADS, nk // 16, 16, PEER_TM), lambda i, j: (0, 0, 0, i))
    flat = pl.BlockSpec((PEER_HEADS, nk, PEER_TM), lambda i, j: (0, 0, i))
    return pl.pallas_call(
        functools.partial(_peer_kernel, gate=gate),
        grid=(t // PEER_TM, PEER_EXPERTS // te),
        in_specs=[
            pl.BlockSpec((PEER_TM, d), lambda i, j: (i, 0)),
            pl.BlockSpec((te, d), lambda i, j: (j, 0)),
            pl.BlockSpec((d, te), lambda i, j: (0, j)),
            slab, slab, flat, flat,
            pl.BlockSpec((PEER_TM, d), lambda i, j: (i, 0)),
            pl.BlockSpec((None, N_MOD, d), lambda i, j: (_group_of_tile(i, PEER_TM), 0, 0)),
        ],
        out_specs=pl.BlockSpec((PEER_TM, d), lambda i, j: (i, 0)),
        out_shape=jax.ShapeDtypeStruct((t, d), F32),
        scratch_shapes=[pltpu.VMEM((d, PEER_TM), F32)],
        compiler_params=_cparams("parallel", "arbitrary"),
        name="peer_dense",
    )(hb, u_bf16, vt_bf16, r2, bw, nsel, aw, x, mods)


def _lower_bound_terms(rec_lb):
    lb_p = jax.nn.softmax(rec_lb.astype(F32), axis=0)
    lb_cs = jnp.cumsum(lb_p, axis=0)
    lb = lb_cs - lb_cs[0]
    terms = jnp.stack([jnp.log(lb), jnp.log1p(-lb), 1.0 - lb], axis=2)
    return jnp.pad(terms, ((0, 0), (0, 0), (0, 5), (0, 0)))


def kernel(x_prompt, x_sample, c, state_rec, cache_k, cache_v, c_ctx, norm_w, ada_w, ada_b,
           rec_w_in, rec_lb, rec_gn_w, rec_w_out, attn_w_qkv, attn_qn_w, attn_kn_w, attn_sink,
           attn_w_out, peer_w_q, peer_keys, peer_u, peer_v):
    d = D_MODEL
    x = jnp.concatenate([x_prompt.reshape(CTX_TOKENS, d), x_sample.reshape(LAT_TOKENS, d)], axis=0)
    cond = jnp.zeros((N_GROUPS, d), F32).at[0].set(c_ctx).at[1:1 + DEC_BATCH].set(c)
    mods_all = _ada_mods(cond, ada_w, ada_b).reshape(DEPTH, N_GROUPS, N_MOD, d)

    lbp = _lower_bound_terms(rec_lb)
    dm, mk = _rec_consts()
    dm = jnp.asarray(dm, BF16)
    mk = jnp.asarray(mk, F32)
    rope = _rope_tables()
    ck = cache_k.reshape(DEC_BATCH, -1, PAST_LEN, KV_COLS)
    cv = cache_v.reshape(DEC_BATCH, -1, PAST_LEN, KV_COLS)

    new_rec, new_k, new_v = [], [], []
    for i in range(DEPTH):
        mods = mods_all[i]
        r = i // 2
        if i % 2 == 0:
            z = _norm_mod_matmul(x, norm_w[i, 0], mods, rec_w_in[r].astype(BF16), 0, 1, 1024)
            o_ctx, st = _rec_scan(z, lbp[r], dm, mk, 0, BATCH, SEQ, None, r, True)
            (o_lat,) = _rec_scan(z, lbp[r], dm, mk, CTX_TOKENS, DEC_BATCH, DEC_SEQ, state_rec, r, False)
            o = jnp.concatenate([o_ctx, o_lat], axis=1)
            gn = jnp.tile(rec_gn_w[r], REC_HEADS).reshape(1, d)
            x = _rec_out(o, z, x, mods, gn, rec_w_out[r].astype(BF16), 2)
            new_rec.append(st)
        else:
            z = _norm_mod_matmul(x, norm_w[i, 0], mods, attn_w_qkv[r].astype(BF16), 0, 1, 512)
            wt = jnp.concatenate([jnp.tile(attn_qn_w[r], ATTN_HEADS), jnp.tile(attn_kn_w[r], ATTN_KV_HEADS)])
            qn, kn = _attn_prep(z, wt.reshape(1, QK_COLS), rope)
            a_ctx = _attn_ctx(attn_sink[r], qn, kn, z)
            a_lat = _attn_lat(attn_sink[r], qn, kn, z, ck, cv, r)
            x = _proj_residual(a_ctx, a_lat, x, mods, attn_w_out[r].astype(BF16), 2)
            new_k.append(kn[:CTX_TOKENS].reshape(BATCH, SEQ, ATTN_KV_HEADS, HEAD_DIM))
            new_v.append(z[:CTX_TOKENS, Q_COLS + KV_COLS:].reshape(BATCH, SEQ, ATTN_KV_HEADS, HEAD_DIM))
        wq_hi = peer_w_q[i].astype(BF16)
        wq_lo = (peer_w_q[i] - wq_hi.astype(F32)).astype(BF16)
        q, hb = _norm_mod_query(x, norm_w[i, 1], mods, wq_hi, wq_lo, 3, 4, 1024)
        k_hi = peer_keys[i].astype(BF16)
        k_lo = (peer_keys[i] - k_hi.astype(F32)).astype(BF16)
        r2, bw, nsel, aw = _peer_route(q, k_hi, k_lo)
        x = _peer_dense(hb, peer_u[i].astype(BF16), peer_v[i].T.astype(BF16), r2, bw, nsel, aw, x, mods, 5)

    y_prompt = x[:CTX_TOKENS].reshape(BATCH, SEQ, d)
    y_sample = x[CTX_TOKENS:].reshape(DEC_BATCH, DEC_SEQ, d)
    return (y_prompt, y_sample, jnp.stack(new_rec, axis=1), jnp.stack(new_k, axis=1), jnp.stack(new_v, axis=1))
```

```python
import functools

import numpy as np
import jax
import jax.numpy as jnp
from jax import lax
from jax.experimental import pallas as pl
from jax.experimental.pallas import tpu as pltpu

F32 = jnp.float32
BF16 = jnp.bfloat16

D_MODEL = 1024
BATCH = 16
SEQ = 256
DEPTH = 4
DEC_BATCH = 8
DEC_SEQ = 2048
PAST_LEN = 512
GRID_W = 64
N_MOD = 6
EPS = 1e-6
NEG_INF = -1e30
REC_HEADS = 8
REC_DK = 128
REC_CHUNK = 64
ATTN_HEADS = 16
ATTN_KV_HEADS = 4
ATTN_GROUPS = 4
HEAD_DIM = 64
WINDOW = 128
ATTN_SCALE = HEAD_DIM ** -0.5
ROPE_BASE = 10000.0
PEER_HEADS = 8
PEER_NKEYS = 128
PEER_EXPERTS = PEER_NKEYS * PEER_NKEYS
PEER_TOPK = 16
PEER_DQ = 256

CTX_TOKENS = BATCH * SEQ
LAT_TOKENS = DEC_BATCH * DEC_SEQ
TOKENS = CTX_TOKENS + LAT_TOKENS
N_GROUPS = 16

V7X_VMEM_LIMIT = 48 * 1024 * 1024
LANES = 128
TM = 512
NOT_RANKED = 99.0


def _cparams(*sem):
    return pltpu.CompilerParams(dimension_semantics=sem, vmem_limit_bytes=V7X_VMEM_LIMIT)


def _group_of_tile(i, tm):
    ctx_tiles = CTX_TOKENS // tm
    per_seq = DEC_SEQ // tm
    return jnp.where(i < ctx_tiles, 0, 1 + jnp.maximum(i - ctx_tiles, 0) // per_seq)


def _split2(a):
    hi = a.astype(BF16)
    lo = (a - hi.astype(F32)).astype(BF16)
    return hi, lo


def _split3(a):
    p1 = a.astype(BF16)
    r1 = a - p1.astype(F32)
    p2 = r1.astype(BF16)
    p3 = (r1 - p2.astype(F32)).astype(BF16)
    return p1, p2, p3


def _dot(a, b):
    return jnp.dot(a, b, preferred_element_type=F32)


def _dot_nt(a, b):
    return lax.dot_general(a, b, (((1,), (1,)), ((), ())), preferred_element_type=F32)


def _dot_tn(a, b):
    return lax.dot_general(a, b, (((0,), (0,)), ((), ())), preferred_element_type=F32)


def _exact_left(sel, a):
    p1, p2, p3 = _split3(a)
    return _dot(sel, p1) + _dot(sel, p2) + _dot(sel, p3)


def _exact_right(a, sel):
    p1, p2, p3 = _split3(a)
    return _dot(p1, sel) + _dot(p2, sel) + _dot(p3, sel)


def _silu(x):
    return x * jax.nn.sigmoid(x)


def _norm_mod(x, nw, shift, scale):
    ms = jnp.mean(x * x, axis=-1, keepdims=True)
    y = x * lax.rsqrt(ms + EPS) * nw
    return y * (1.0 + scale) + shift


def _ada_kernel(cond_ref, w_ref, b_ref, o_ref):
    a = _silu(cond_ref[...])
    a_hi, a_lo = _split2(a)
    w_hi, w_lo = _split2(w_ref[...])
    o_ref[...] = _dot(a_hi, w_hi) + _dot(a_hi, w_lo) + _dot(a_lo, w_hi) + b_ref[...]


def _ada_mods(cond, ada_w, ada_b):
    tn = 1024
    n = N_MOD * D_MODEL
    return pl.pallas_call(
        _ada_kernel,
        grid=(DEPTH, n // tn),
        in_specs=[
            pl.BlockSpec((N_GROUPS, D_MODEL), lambda l, j: (0, 0)),
            pl.BlockSpec((None, D_MODEL, tn), lambda l, j: (l, 0, j)),
            pl.BlockSpec((None, 1, tn), lambda l, j: (l, 0, j)),
        ],
        out_specs=pl.BlockSpec((None, N_GROUPS, tn), lambda l, j: (l, 0, j)),
        out_shape=jax.ShapeDtypeStruct((DEPTH, N_GROUPS, n), F32),
        compiler_params=_cparams("parallel", "parallel"),
        name="ada_mods",
    )(cond, ada_w, ada_b.reshape(DEPTH, 1, n))


def _nm_kernel(x_ref, nw_ref, mod_ref, w_ref, o_ref, h_scr, *, shift, scale):
    @pl.when(pl.program_id(1) == 0)
    def _():
        h = _norm_mod(x_ref[...], nw_ref[...], mod_ref[shift:shift + 1, :], mod_ref[scale:scale + 1, :])
        h_scr[...] = h.astype(BF16)

    o_ref[...] = _dot(h_scr[...], w_ref[...])


def _norm_mod_matmul(x, nw, mods, w_bf16, shift, scale, tn):
    t, d = x.shape
    n = w_bf16.shape[1]
    return pl.pallas_call(
        functools.partial(_nm_kernel, shift=shift, scale=scale),
        grid=(t // TM, n // tn),
        in_specs=[
            pl.BlockSpec((TM, d), lambda i, j: (i, 0)),
            pl.BlockSpec((1, d), lambda i, j: (0, 0)),
            pl.BlockSpec((None, N_MOD, d), lambda i, j: (_group_of_tile(i, TM), 0, 0)),
            pl.BlockSpec((d, tn), lambda i, j: (0, j)),
        ],
        out_specs=pl.BlockSpec((TM, tn), lambda i, j: (i, j)),
        out_shape=jax.ShapeDtypeStruct((t, n), F32),
        scratch_shapes=[pltpu.VMEM((TM, d), BF16)],
        compiler_params=_cparams("parallel", "arbitrary"),
        name="norm_mod_matmul",
    )(x, nw.reshape(1, d), mods, w_bf16)


def _nm_query_kernel(x_ref, nw_ref, mod_ref, whi_ref, wlo_ref, q_ref, hb_ref, hhi_scr, hlo_scr, *, shift, scale):
    @pl.when(pl.program_id(1) == 0)
    def _():
        h = _norm_mod(x_ref[...], nw_ref[...], mod_ref[shift:shift + 1, :], mod_ref[scale:scale + 1, :])
        hi, lo = _split2(h)
        hhi_scr[...] = hi
        hlo_scr[...] = lo
        hb_ref[...] = hi

    hi = hhi_scr[...]
    q_ref[...] = _dot(hi, whi_ref[...]) + _dot(hi, wlo_ref[...]) + _dot(hlo_scr[...], whi_ref[...])


def _norm_mod_query(x, nw, mods, w_hi, w_lo, shift, scale, tn):
    t, d = x.shape
    n = w_hi.shape[1]
    return pl.pallas_call(
        functools.partial(_nm_query_kernel, shift=shift, scale=scale),
        grid=(t // TM, n // tn),
        in_specs=[
            pl.BlockSpec((TM, d), lambda i, j: (i, 0)),
            pl.BlockSpec((1, d), lambda i, j: (0, 0)),
            pl.BlockSpec((None, N_MOD, d), lambda i, j: (_group_of_tile(i, TM), 0, 0)),
            pl.BlockSpec((d, tn), lambda i, j: (0, j)),
            pl.BlockSpec((d, tn), lambda i, j: (0, j)),
        ],
        out_specs=[
            pl.BlockSpec((TM, tn), lambda i, j: (i, j)),
            pl.BlockSpec((TM, d), lambda i, j: (i, 0)),
        ],
        out_shape=[jax.ShapeDtypeStruct((t, n), F32), jax.ShapeDtypeStruct((t, d), BF16)],
        scratch_shapes=[pltpu.VMEM((TM, d), BF16), pltpu.VMEM((TM, d), BF16)],
        compiler_params=_cparams("parallel", "arbitrary"),
        name="norm_mod_query",
    )(x, nw.reshape(1, d), mods, w_hi, w_lo)


REC_LEVELS = (32, 16, 8, 4, 2, 1)
REC_STEP = 256
SUBLANES = 8


def _mirror(rows, backward):
    return REC_CHUNK - 1 - rows if backward else rows


def _rec_consts(backward):
    c = REC_CHUNK
    cs = np.zeros((c + SUBLANES, c), np.float32)
    mk = np.zeros((len(REC_LEVELS) + 1, c, c), np.float32)
    for t in range(c):
        cs[t, :t + 1] = 1.0
    for lvl, half in enumerate(REC_LEVELS):
        blk = 2 * half
        for t in range(c):
            for s in range(c):
                if t // blk == s // blk and t % blk >= half and s % blk < half:
                    mk[lvl, t, s] = 1.0
    mk[len(REC_LEVELS)] = np.eye(c)
    if backward:
        cs[:c] = cs[:c][::-1, ::-1]
        mk = mk[:, ::-1, ::-1]
    cs[c:] = 1.0
    return cs, np.ascontiguousarray(mk)


def _level_exponent(b, half, backward):
    sub = lax.broadcasted_iota(jnp.int32, (SUBLANES, REC_DK), 0)
    parts = []
    for g in range(REC_CHUNK // SUBLANES):
        rows = np.arange(g * SUBLANES, (g + 1) * SUBLANES)
        mirrored = _mirror(rows, backward)
        ref = _mirror((mirrored // (2 * half)) * (2 * half) + half, backward)
        bref = None
        for u in np.unique(ref):
            row = jnp.broadcast_to(b[u:u + 1], (SUBLANES, REC_DK))
            if bref is None:
                bref = row
            else:
                where_u = np.nonzero(ref == u)[0]
                bref = jnp.where((sub >= int(where_u.min())) & (sub <= int(where_u.max())), row, bref)
        parts.append(-jnp.abs(b[g * SUBLANES:(g + 1) * SUBLANES] - bref))
    return jnp.concatenate(parts, axis=0)


def _rec_scan_kernel(*refs, backward, zero_init, emit_state):
    zq_ref, zf_ref, zi_ref, lbp_ref, cs_ref, mk_ref = refs[:6]
    pos = 6
    if not zero_init:
        s0_ref = refs[pos]
        pos += 1
    o_ref = refs[pos]
    pos += 1
    if emit_state:
        sout_ref = refs[pos]
        pos += 1
    st_scr = refs[pos]

    step = pl.program_id(1)
    nchunk = REC_STEP // REC_CHUNK

    @pl.when(step == 0)
    def _():
        for h in range(REC_HEADS):
            if zero_init:
                st_scr[h] = jnp.zeros((REC_DK, REC_DK), F32)
            else:
                st_scr[h] = s0_ref[h].T

    csum = cs_ref[...]

    def chunk_body(ci, carry):
        cc = nchunk - 1 - ci if backward else ci
        rows = pl.ds(pl.multiple_of(cc * REC_CHUNK, REC_CHUNK), REC_CHUNK)
        for h in range(REC_HEADS):
            cols = slice(h * REC_DK, (h + 1) * REC_DK)
            q = _silu(zq_ref[rows, cols])
            fz = zf_ref[rows, cols]
            v = zi_ref[rows, cols].astype(BF16)
            log_lb = lbp_ref[0:1, cols]
            log_1m = lbp_ref[1:2, cols]
            one_m = lbp_ref[2:3, cols]
            log_sig = jnp.minimum(fz, 0.0) - jnp.log1p(jnp.exp(-jnp.abs(fz)))
            cterm = log_1m + log_sig
            logf = jnp.maximum(log_lb, cterm) + jnp.log1p(jnp.exp(-jnp.abs(log_lb - cterm)))
            k = one_m * jax.nn.sigmoid(-fz)

            f_hi, f_lo = _split2(logf)
            bsum = _dot(csum, f_hi) + _dot(csum, f_lo)
            b = bsum[:REC_CHUNK]
            total = bsum[REC_CHUNK:REC_CHUNK + 1]
            sc = mk_ref[len(REC_LEVELS)] * _dot_nt(q.astype(BF16), k.astype(BF16))
            for lvl, half in enumerate(REC_LEVELS):
                e = jnp.exp(_level_exponent(b, half, backward))
                sc = sc + mk_ref[lvl] * _dot_nt((q * e).astype(BF16), (k * e).astype(BF16))
            st = st_scr[h]
            qb = (q * jnp.exp(b)).astype(BF16)
            o = _dot(sc.astype(BF16), v) + _dot_nt(qb, st.astype(BF16))
            o_ref[rows, cols] = o
            kd = (k * jnp.exp(total - b)).astype(BF16)
            st_scr[h] = st * jnp.exp(total) + _dot_tn(v, kd)
        return carry

    lax.fori_loop(0, nchunk, chunk_body, 0)

    if emit_state:
        @pl.when(step == pl.num_programs(1) - 1)
        def _():
            for h in range(REC_HEADS):
                sout_ref[h] = st_scr[h].T


def _rec_scan(z, lbp, backward, row0, nseq, seqlen, s0, layer_r, emit_state):
    steps = seqlen // REC_STEP
    blk0 = row0 // REC_STEP
    d = D_MODEL
    dr = 1 if backward else 0
    cs, mk = _rec_consts(backward)

    def seq_blk(b, c):
        return b * steps + (steps - 1 - c if backward else c)

    in_specs = [
        pl.BlockSpec((REC_STEP, d), lambda b, c: (blk0 + seq_blk(b, c), 0)),
        pl.BlockSpec((REC_STEP, d), lambda b, c: (blk0 + seq_blk(b, c), 1 + dr)),
        pl.BlockSpec((REC_STEP, d), lambda b, c: (blk0 + seq_blk(b, c), 3)),
        pl.BlockSpec((None, 8, d), lambda b, c: (dr, 0, 0)),
        pl.BlockSpec(cs.shape, lambda b, c: (0, 0)),
        pl.BlockSpec(mk.shape, lambda b, c: (0, 0, 0)),
    ]
    args = [z, z, z, lbp, jnp.asarray(cs, BF16), jnp.asarray(mk, F32)]
    if s0 is not None:
        in_specs.append(pl.BlockSpec((None, None, None, REC_HEADS, REC_DK, REC_DK),
                                     lambda b, c: (b, layer_r, dr, 0, 0, 0)))
        args.append(s0)
    out_specs = [pl.BlockSpec((REC_STEP, d), lambda b, c: (seq_blk(b, c), 0))]
    out_shape = [jax.ShapeDtypeStruct((nseq * seqlen, d), F32)]
    if emit_state:
        out_specs.append(pl.BlockSpec((None, REC_HEADS, REC_DK, REC_DK), lambda b, c: (b, 0, 0, 0)))
        out_shape.append(jax.ShapeDtypeStruct((nseq, REC_HEADS, REC_DK, REC_DK), F32))
    return pl.pallas_call(
        functools.partial(_rec_scan_kernel, backward=backward, zero_init=s0 is None, emit_state=emit_state),
        grid=(nseq, steps),
        in_specs=in_specs,
        out_specs=out_specs,
        out_shape=out_shape,
        scratch_shapes=[pltpu.VMEM((REC_HEADS, REC_DK, REC_DK), F32)],
        compiler_params=_cparams("parallel", "arbitrary"),
        name="rec_scan",
    )(*args)


def _rec_out_kernel(ocf_ref, ocb_ref, olf_ref, olb_ref, zg_ref, x_ref, mod_ref, gn_ref, w_ref, out_ref, y_scr, *, gate):
    def gated_norm(of_ref, ob_ref):
        for h in range(REC_HEADS):
            cols = slice(h * REC_DK, (h + 1) * REC_DK)
            o = of_ref[:, cols] + ob_ref[:, cols]
            ms = jnp.mean(o * o, axis=-1, keepdims=True)
            y = o * lax.rsqrt(ms + EPS) * gn_ref[:, cols] * _silu(zg_ref[:, cols])
            y_scr[:, cols] = y.astype(BF16)

    is_ctx = pl.program_id(0) < CTX_TOKENS // TM
    pl.when(is_ctx)(lambda: gated_norm(ocf_ref, ocb_ref))
    pl.when(jnp.logical_not(is_ctx))(lambda: gated_norm(olf_ref, olb_ref))
    out_ref[...] = x_ref[...] + mod_ref[gate:gate + 1, :] * _dot(y_scr[...], w_ref[...])


def _rec_out(o_ctx, o_lat, z, x, mods, gn_tiled, w_out_bf16, gate):
    t, d = x.shape
    ctx_tiles = CTX_TOKENS // TM
    ctx_blk = lambda i: (jnp.minimum(i, ctx_tiles - 1), 0)
    lat_blk = lambda i: (jnp.maximum(i - ctx_tiles, 0), 0)
    return pl.pallas_call(
        functools.partial(_rec_out_kernel, gate=gate),
        grid=(t // TM,),
        in_specs=[
            pl.BlockSpec((TM, d), ctx_blk),
            pl.BlockSpec((TM, d), ctx_blk),
            pl.BlockSpec((TM, d), lat_blk),
            pl.BlockSpec((TM, d), lat_blk),
            pl.BlockSpec((TM, d), lambda i: (i, 4)),
            pl.BlockSpec((TM, d), lambda i: (i, 0)),
            pl.BlockSpec((None, N_MOD, d), lambda i: (_group_of_tile(i, TM), 0, 0)),
            pl.BlockSpec((1, d), lambda i: (0, 0)),
            pl.BlockSpec((d, d), lambda i: (0, 0)),
        ],
        out_specs=pl.BlockSpec((TM, d), lambda i: (i, 0)),
        out_shape=jax.ShapeDtypeStruct((t, d), F32),
        scratch_shapes=[pltpu.VMEM((TM, d), BF16)],
        compiler_params=_cparams("parallel"),
        name="rec_out",
    )(*o_ctx, *o_lat, z, x, mods, gn_tiled, w_out_bf16)


QK_COLS = (ATTN_HEADS + ATTN_KV_HEADS) * HEAD_DIM
Q_COLS = ATTN_HEADS * HEAD_DIM
KV_COLS = ATTN_KV_HEADS * HEAD_DIM
ROPE_SHIFT = HEAD_DIM // 4


def _attn_prep_kernel(z_ref, wt_ref, gm_ref, gmt_ref, cos_ref, sa_ref, sb_ref, q_ref, k_ref):
    x = z_ref[...]
    ss = _exact_right(x * x, gm_ref[...])
    r = lax.rsqrt(ss * (1.0 / HEAD_DIM) + EPS)
    y = x * _exact_right(r, gmt_ref[...]) * wt_ref[...]
    is_lat = pl.program_id(0) >= CTX_TOKENS // TM

    @pl.when(is_lat)
    def _():
        cos, sa, sb = cos_ref[...], sa_ref[...], sb_ref[...]
        for c in range(QK_COLS // LANES):
            yc = y[:, c * LANES:(c + 1) * LANES]
            rc = (yc * cos + pltpu.roll(yc, LANES - ROPE_SHIFT, axis=1) * sa
                  + pltpu.roll(yc, ROPE_SHIFT, axis=1) * sb)
            if c < Q_COLS // LANES:
                q_ref[:, c * LANES:(c + 1) * LANES] = rc.astype(BF16)
            else:
                k_ref[:, c * LANES - Q_COLS:(c + 1) * LANES - Q_COLS] = rc

    @pl.when(jnp.logical_not(is_lat))
    def _():
        q_ref[...] = y[:, :Q_COLS].astype(BF16)
        k_ref[...] = y[:, Q_COLS:]


def _rope_tables():
    half = HEAD_DIM // 2
    nf = half // 2
    pos = jnp.arange(DEC_SEQ)
    row = (pos // GRID_W).astype(F32)
    col = (pos % GRID_W).astype(F32)
    freqs = ROPE_BASE ** (-jnp.arange(nf, dtype=F32) / nf)
    ang_r = row[:, None] * freqs[None, :]
    ang_c = col[:, None] * freqs[None, :]
    zeros = jnp.zeros_like(ang_r)
    cos = jnp.concatenate([jnp.cos(ang_r), jnp.cos(ang_r), jnp.cos(ang_c), jnp.cos(ang_c)], axis=-1)
    sa = jnp.concatenate([-jnp.sin(ang_r), zeros, -jnp.sin(ang_c), zeros], axis=-1)
    sb = jnp.concatenate([zeros, jnp.sin(ang_r), zeros, jnp.sin(ang_c)], axis=-1)
    tile = lambda a: jnp.tile(a, (1, LANES // HEAD_DIM))
    return tile(cos), tile(sa), tile(sb)


def _attn_prep(z, wt, rope):
    t = z.shape[0]
    gm = np.zeros((QK_COLS, LANES), np.float32)
    gm[np.arange(QK_COLS), np.arange(QK_COLS) // HEAD_DIM] = 1.0
    ctx_tiles = CTX_TOKENS // TM
    per_seq = DEC_SEQ // TM
    pos_blk = lambda i: (jnp.maximum(i - ctx_tiles, 0) % per_seq, 0)
    return pl.pallas_call(
        _attn_prep_kernel,
        grid=(t // TM,),
        in_specs=[
            pl.BlockSpec((TM, QK_COLS), lambda i: (i, 0)),
            pl.BlockSpec((1, QK_COLS), lambda i: (0, 0)),
            pl.BlockSpec((QK_COLS, LANES), lambda i: (0, 0)),
            pl.BlockSpec((LANES, QK_COLS), lambda i: (0, 0)),
            pl.BlockSpec((TM, LANES), pos_blk),
            pl.BlockSpec((TM, LANES), pos_blk),
            pl.BlockSpec((TM, LANES), pos_blk),
        ],
        out_specs=[
            pl.BlockSpec((TM, Q_COLS), lambda i: (i, 0)),
            pl.BlockSpec((TM, KV_COLS), lambda i: (i, 0)),
        ],
        out_shape=[jax.ShapeDtypeStruct((t, Q_COLS), BF16), jax.ShapeDtypeStruct((t, KV_COLS), F32)],
        compiler_params=_cparams("parallel"),
        name="attn_prep",
    )(z, wt, jnp.asarray(gm, BF16), jnp.asarray(gm.T, BF16), *rope)


def _attn_kernel(*refs, local, tq):
    sink_ref, q_ref = refs[0], refs[1]
    if local:
        kl_refs = refs[2:5]
        vl_refs = refs[5:8]
        kc_ref, vc_ref, o_ref = refs[8:11]
        j = pl.program_id(1)
        nblk = pl.num_programs(1)
    else:
        kc_ref, vc_ref, o_ref = refs[2:5]
    rows = ATTN_GROUPS * tq
    row_id = lax.broadcasted_iota(jnp.int32, (rows, 1), 0)
    if local:
        t_id = lax.broadcasted_iota(jnp.int32, (rows, 3 * WINDOW), 0) % tq
        c_id = lax.broadcasted_iota(jnp.int32, (rows, 3 * WINDOW), 1)
        rel = c_id - WINDOW - t_id
        kpos = (j - 1) * WINDOW + c_id
        valid = (jnp.abs(rel) <= WINDOW) & (kpos >= 0) & (kpos < nblk * WINDOW)
    outs = []
    for h in range(ATTN_KV_HEADS):
        cols = slice(h * HEAD_DIM, (h + 1) * HEAD_DIM)
        qs = jnp.concatenate(
            [q_ref[:, (h * ATTN_GROUPS + g) * HEAD_DIM:(h * ATTN_GROUPS + g + 1) * HEAD_DIM]
             for g in range(ATTN_GROUPS)], axis=0)
        sink = jnp.zeros((rows, 1), F32)
        for g in range(ATTN_GROUPS):
            sink = jnp.where(row_id // tq == g, sink_ref[h * ATTN_GROUPS + g], sink)
        s_ctx = _dot_nt(qs, kc_ref[:, cols].astype(BF16)) * ATTN_SCALE
        m = jnp.maximum(jnp.max(s_ctx, axis=-1, keepdims=True), sink)
        if local:
            kl = jnp.concatenate([r[:, cols] for r in kl_refs], axis=0).astype(BF16)
            vl = jnp.concatenate([r[:, cols] for r in vl_refs], axis=0).astype(BF16)
            s_loc = jnp.where(valid, _dot_nt(qs, kl) * ATTN_SCALE, NEG_INF)
            m = jnp.maximum(m, jnp.max(s_loc, axis=-1, keepdims=True))
        p_ctx = jnp.exp(s_ctx - m)
        den = jnp.sum(p_ctx, axis=-1, keepdims=True) + jnp.exp(sink - m)
        acc = _dot(p_ctx.astype(BF16), vc_ref[:, cols].astype(BF16))
        if local:
            p_loc = jnp.exp(s_loc - m)
            den = den + jnp.sum(p_loc, axis=-1, keepdims=True)
            acc = acc + _dot(p_loc.astype(BF16), vl)
        o = acc / den
        outs.extend(o[g * tq:(g + 1) * tq] for g in range(ATTN_GROUPS))
    o_ref[...] = jnp.concatenate(outs, axis=-1).astype(BF16)


def _attn_ctx(sink, qn, kn, z):
    vblk = Q_COLS // KV_COLS + 1
    return pl.pallas_call(
        functools.partial(_attn_kernel, local=False, tq=SEQ),
        grid=(BATCH,),
        in_specs=[
            pl.BlockSpec(memory_space=pltpu.SMEM),
            pl.BlockSpec((SEQ, Q_COLS), lambda b: (b, 0)),
            pl.BlockSpec((SEQ, KV_COLS), lambda b: (b, 0)),
            pl.BlockSpec((SEQ, KV_COLS), lambda b: (b, vblk)),
        ],
        out_specs=pl.BlockSpec((SEQ, Q_COLS), lambda b: (b, 0)),
        out_shape=jax.ShapeDtypeStruct((CTX_TOKENS, Q_COLS), BF16),
        compiler_params=_cparams("parallel"),
        name="attn_ctx",
    )(sink, qn, kn, z)


def _attn_lat(sink, qn, kn, z, cache_k, cache_v, layer_r):
    nblk = DEC_SEQ // WINDOW
    blk0 = CTX_TOKENS // WINDOW
    vblk = Q_COLS // KV_COLS + 1

    def nbr(off):
        return lambda b, j: (blk0 + b * nblk + jnp.clip(j + off, 0, nblk - 1), 0)

    def nbr_v(off):
        return lambda b, j: (blk0 + b * nblk + jnp.clip(j + off, 0, nblk - 1), vblk)

    return pl.pallas_call(
        functools.partial(_attn_kernel, local=True, tq=WINDOW),
        grid=(DEC_BATCH, nblk),
        in_specs=[
            pl.BlockSpec(memory_space=pltpu.SMEM),
            pl.BlockSpec((WINDOW, Q_COLS), lambda b, j: (blk0 + b * nblk + j, 0)),
            pl.BlockSpec((WINDOW, KV_COLS), nbr(-1)),
            pl.BlockSpec((WINDOW, KV_COLS), nbr(0)),
            pl.BlockSpec((WINDOW, KV_COLS), nbr(1)),
            pl.BlockSpec((WINDOW, KV_COLS), nbr_v(-1)),
            pl.BlockSpec((WINDOW, KV_COLS), nbr_v(0)),
            pl.BlockSpec((WINDOW, KV_COLS), nbr_v(1)),
            pl.BlockSpec((None, None, PAST_LEN, KV_COLS), lambda b, j: (b, layer_r, 0, 0)),
            pl.BlockSpec((None, None, PAST_LEN, KV_COLS), lambda b, j: (b, layer_r, 0, 0)),
        ],
        out_specs=pl.BlockSpec((WINDOW, Q_COLS), lambda b, j: (b * nblk + j, 0)),
        out_shape=jax.ShapeDtypeStruct((LAT_TOKENS, Q_COLS), BF16),
        compiler_params=_cparams("parallel", "parallel"),
        name="attn_lat",
    )(sink, qn, kn, kn, kn, z, z, z, cache_k, cache_v)


def _proj_res_kernel(a_ref, x_ref, mod_ref, w_ref, out_ref, *, gate):
    out_ref[...] = x_ref[...] + mod_ref[gate:gate + 1, :] * _dot(a_ref[...], w_ref[...])


def _proj_residual(a_ctx, a_lat, x, mods, w_bf16, gate):
    t, d = x.shape
    ctx_tiles = CTX_TOKENS // TM

    def kernel(ac_ref, al_ref, x_ref, mod_ref, w_ref, out_ref):
        i = pl.program_id(0)

        @pl.when(i < ctx_tiles)
        def _():
            _proj_res_kernel(ac_ref, x_ref, mod_ref, w_ref, out_ref, gate=gate)

        @pl.when(i >= ctx_tiles)
        def _():
            _proj_res_kernel(al_ref, x_ref, mod_ref, w_ref, out_ref, gate=gate)

    return pl.pallas_call(
        kernel,
        grid=(t // TM,),
        in_specs=[
            pl.BlockSpec((TM, d), lambda i: (jnp.minimum(i, ctx_tiles - 1), 0)),
            pl.BlockSpec((TM, d), lambda i: (jnp.maximum(i - ctx_tiles, 0), 0)),
            pl.BlockSpec((TM, d), lambda i: (i, 0)),
            pl.BlockSpec((None, N_MOD, d), lambda i: (_group_of_tile(i, TM), 0, 0)),
            pl.BlockSpec((d, d), lambda i: (0, 0)),
        ],
        out_specs=pl.BlockSpec((TM, d), lambda i: (i, 0)),
        out_shape=jax.ShapeDtypeStruct((t, d), F32),
        compiler_params=_cparams("parallel"),
        name="proj_residual",
    )(a_ctx, a_lat, x, mods, w_bf16)


ROUTE_TM = 512
CAND_ROWS = 16 + 8 + 6 * 8 + 8
MARK_SCALE = 2.0 ** 96
MARK_BASE = 64.0


def _extract_marked(work, rounds):
    vals = []
    for r in range(rounds):
        m = jnp.max(work, axis=0, keepdims=True)
        work = jnp.where(work == m, jnp.float32(-(MARK_BASE + r) * MARK_SCALE), work)
        vals.append(m)
    return vals, work


def _marked(work, rounds):
    taken = (work <= -MARK_BASE * MARK_SCALE) & (work >= -(MARK_BASE + rounds) * MARK_SCALE)
    rank = jnp.where(taken, work * (-1.0 / MARK_SCALE) - MARK_BASE, NOT_RANKED)
    count = jnp.sum(jnp.where(taken, 1.0, 0.0), axis=0, keepdims=True)
    return taken, rank, count


def _candidate_sums(v1, v2, tm):
    k = PEER_TOPK
    sub = lax.broadcasted_iota(jnp.int32, (8, tm), 0).astype(F32)
    sub16 = lax.broadcasted_iota(jnp.int32, (k, tm), 0).astype(F32)
    s2s = jnp.concatenate(v2, axis=0)
    s1s_hi = jnp.concatenate(v1[8:], axis=0)
    cand = [v1[0] + s2s, v1[1] + s2s[:8]]
    order = [sub16, k + sub]
    for j1 in range(2, 8):
        cnt = k // (j1 + 1)
        cand.append(jnp.where(sub < cnt, v1[j1] + s2s[:8], -jnp.inf))
        order.append(j1 * k + sub)
    cand.append(s1s_hi + v2[0])
    order.append((8 + sub) * k)
    return jnp.concatenate(cand, axis=0), jnp.concatenate(order, axis=0)


def _route_head_fast(s1, s2):
    k = PEER_TOPK
    tm = s1.shape[1]
    v1, w1 = _extract_marked(s1, k)
    v2, w2 = _extract_marked(s2, k)
    _, rank1, c1 = _marked(w1, k)
    _, rank2, c2 = _marked(w2, k)
    cand, _ = _candidate_sums(v1, v2, tm)
    best, wc = _extract_marked(cand, k)
    taken, _, c3 = _marked(wc, k)
    tk = jnp.where(taken, 1.0, 0.0)
    nr = [jnp.sum(tk[0:16], axis=0, keepdims=True), jnp.sum(tk[16:24], axis=0, keepdims=True)]
    nr += [jnp.sum(tk[8 * j1 + 8:8 * j1 + 16], axis=0, keepdims=True) for j1 in range(2, 8)]
    nr += [tk[72 + i:73 + i] for i in range(8)]
    rank1b = rank1.astype(BF16).reshape(PEER_NKEYS // 16, 16, tm)
    nsel = jnp.zeros(rank1b.shape, BF16)
    for j1 in range(k):
        cnt = jnp.broadcast_to(nr[j1], (16, tm)).astype(BF16)
        nsel = jnp.where(rank1b == j1, cnt[None], nsel)
    z = jnp.zeros((1, tm), F32)
    for r in range(k):
        z = z + jnp.exp(best[r] - best[0])
    a = jnp.exp(s1 - v1[0]) / z
    b = jnp.exp(s2 - v2[0])
    bad = jnp.where((c1 != k) | (c2 != k) | (c3 != k), 1.0, 0.0)
    return rank2, b, nsel.reshape(PEER_NKEYS, tm).astype(F32), a, bad


def _extract_topk(work, order, rounds):
    rank = jnp.full(work.shape, NOT_RANKED, F32)
    vals, ords = [], []
    big = jnp.float32(1e9)
    for r in range(rounds):
        m = jnp.max(work, axis=0, keepdims=True)
        o = jnp.min(jnp.where(work == m, order, big), axis=0, keepdims=True)
        sel = order == o
        rank = jnp.where(sel, jnp.float32(r), rank)
        work = jnp.where(sel, -jnp.inf, work)
        vals.append(m)
        ords.append(o)
    return vals, ords, rank


def _route_head_exact(s1, s2):
    k = PEER_TOPK
    tm = s1.shape[1]
    key_id = lax.broadcasted_iota(jnp.int32, (PEER_NKEYS, tm), 0).astype(F32)
    v1, _, rank1 = _extract_topk(s1, key_id, k)
    v2, _, rank2 = _extract_topk(s2, key_id, k)
    cand, order = _candidate_sums(v1, v2, tm)
    best, best_ord, _ = _extract_topk(cand, order, k)
    z = jnp.zeros((1, tm), F32)
    nsel = jnp.zeros((PEER_NKEYS, tm), F32)
    for r in range(k):
        z = z + jnp.exp(best[r] - best[0])
        j1 = jnp.floor(best_ord[r] * (1.0 / k))
        nsel = nsel + jnp.where(rank1 == j1, 1.0, 0.0)
    a = jnp.exp(s1 - v1[0]) / z
    b = jnp.exp(s2 - v2[0])
    return rank2, b, nsel, a


def _route_kernel(q_ref, khi_ref, klo_ref, r2_ref, b_ref, n_ref, a_ref):
    tm = q_ref.shape[0]

    def sub_scores(p):
        q_hi, q_lo = _split2(q_ref[:, p * LANES:(p + 1) * LANES])
        k_hi, k_lo = khi_ref[p], klo_ref[p]
        return _dot_nt(k_hi, q_hi) + _dot_nt(k_hi, q_lo) + _dot_nt(k_lo, q_hi)

    def emit(rank2, b, nsel, a):
        r2_ref[...] = rank2.astype(BF16).reshape(PEER_NKEYS // 16, 16, tm)
        b_ref[...] = b.astype(BF16).reshape(PEER_NKEYS // 16, 16, tm)
        n_ref[...] = nsel
        a_ref[...] = a

    rank2, b, nsel, a, bad = _route_head_fast(sub_scores(0), sub_scores(1))
    emit(rank2, b, nsel, a)

    @pl.when(jnp.max(bad) > 0.0)
    def _():
        emit(*_route_head_exact(sub_scores(0), sub_scores(1)))


def _peer_route(q, k_hi, k_lo):
    t = q.shape[0]
    nk = PEER_NKEYS
    slab = pl.BlockSpec((None, nk // 16, 16, ROUTE_TM), lambda i, h: (h, 0, 0, i))
    flat = pl.BlockSpec((None, nk, ROUTE_TM), lambda i, h: (h, 0, i))
    return pl.pallas_call(
        _route_kernel,
        grid=(t // ROUTE_TM, PEER_HEADS),
        in_specs=[
            pl.BlockSpec((ROUTE_TM, PEER_DQ), lambda i, h: (i, h)),
            pl.BlockSpec((None, 2, nk, LANES), lambda i, h: (h, 0, 0, 0)),
            pl.BlockSpec((None, 2, nk, LANES), lambda i, h: (h, 0, 0, 0)),
        ],
        out_specs=[slab, slab, flat, flat],
        out_shape=[
            jax.ShapeDtypeStruct((PEER_HEADS, nk // 16, 16, t), BF16),
            jax.ShapeDtypeStruct((PEER_HEADS, nk // 16, 16, t), BF16),
            jax.ShapeDtypeStruct((PEER_HEADS, nk, t), F32),
            jax.ShapeDtypeStruct((PEER_HEADS, nk, t), F32),
        ],
        compiler_params=_cparams("parallel", "parallel"),
        name="peer_route",
    )(q, k_hi, k_lo)


PEER_TM = 512
PEER_ROWS = 4
INV_SQRT2 = 0.7071067811865476


PEER_TE = PEER_ROWS * PEER_NKEYS
PEER_BLOCKS = PEER_EXPERTS // PEER_TE
PEER_SKEW = 1
PEER_SPLIT = 2


def _peer_kernel(hb_ref, u_ref, vt_ref, r2_ref, b_ref, n_ref, a_ref, x_ref, mod_ref, out_ref,
                 acc_ref, ht0_ref, ht1_ref, *, gate):
    j = pl.program_id(1)

    @pl.when(j == 0)
    def _():
        acc_ref[...] = jnp.zeros_like(acc_ref)
        ht1_ref[...] = jnp.zeros_like(ht1_ref)

    def stages(ht_w, ht_r):
        ht_w[...] = _dot_nt(u_ref[...], hb_ref[...])
        live = j >= 1
        blk = jnp.maximum(j - 1, 0)
        rows_per_slab = PEER_ROWS // PEER_SPLIT
        for s in range(PEER_SPLIT):
            parts = []
            for i in range(s * rows_per_slab, (s + 1) * rows_per_slab):
                row = blk * PEER_ROWS + i
                g = jnp.zeros((PEER_NKEYS // 16, 16, PEER_TM), BF16)
                for h in range(PEER_HEADS):
                    cnt = jnp.where(live, n_ref[h, pl.ds(row, 1), :], 0.0)
                    cnt = jnp.broadcast_to(cnt, (16, PEER_TM)).astype(BF16)
                    wgt = jnp.broadcast_to(a_ref[h, pl.ds(row, 1), :], (16, PEER_TM)).astype(BF16)
                    g = g + jnp.where(r2_ref[h] < cnt[None], b_ref[h] * wgt[None], jnp.zeros((), BF16))
                hh = ht_r[i * PEER_NKEYS:(i + 1) * PEER_NKEYS, :]
                act = 0.5 * hh * (1.0 + lax.erf(hh * INV_SQRT2))
                parts.append(g.reshape(PEER_NKEYS, PEER_TM) * act.astype(BF16))
            slab = slice(s * rows_per_slab * PEER_NKEYS, (s + 1) * rows_per_slab * PEER_NKEYS)
            acc_ref[...] += _dot(vt_ref[:, slab], jnp.concatenate(parts, axis=0))

    pl.when(j % 2 == 0)(lambda: stages(ht0_ref, ht1_ref))
    pl.when(j % 2 == 1)(lambda: stages(ht1_ref, ht0_ref))

    @pl.when(j == pl.num_programs(1) - 1)
    def _():
        out_ref[...] = x_ref[...] + mod_ref[gate:gate + 1, :] * acc_ref[...].T


def _peer_dense(hb, u_bf16, vt_bf16, r2, bw, nsel, aw, x, mods, gate):
    t, d = x.shape
    nk = PEER_NKEYS
    last = PEER_BLOCKS - 1
    slab = pl.BlockSpec((PEER_HEADS, nk // 16, 16, PEER_TM), lambda i, j: (0, 0, 0, i))
    flat = pl.BlockSpec((PEER_HEADS, nk, PEER_TM), lambda i, j: (0, 0, i))
    return pl.pallas_call(
        functools.partial(_peer_kernel, gate=gate),
        grid=(t // PEER_TM, PEER_BLOCKS + PEER_SKEW),
        in_specs=[
            pl.BlockSpec((PEER_TM, d), lambda i, j: (i, 0)),
            pl.BlockSpec((PEER_TE, d), lambda i, j: (jnp.minimum(j, last), 0)),
            pl.BlockSpec((d, PEER_TE), lambda i, j: (0, jnp.clip(j - PEER_SKEW, 0, last))),
            slab, slab, flat, flat,
            pl.BlockSpec((PEER_TM, d), lambda i, j: (i, 0)),
            pl.BlockSpec((None, N_MOD, d), lambda i, j: (_group_of_tile(i, PEER_TM), 0, 0)),
        ],
        out_specs=pl.BlockSpec((PEER_TM, d), lambda i, j: (i, 0)),
        out_shape=jax.ShapeDtypeStruct((t, d), F32),
        scratch_shapes=[
            pltpu.VMEM((d, PEER_TM), F32),
            pltpu.VMEM((PEER_TE, PEER_TM), F32), pltpu.VMEM((PEER_TE, PEER_TM), F32),
        ],
        compiler_params=_cparams("parallel", "arbitrary"),
        name="peer_dense",
    )(hb, u_bf16, vt_bf16, r2, bw, nsel, aw, x, mods)


def _lower_bound_terms(rec_lb):
    lb_p = jax.nn.softmax(rec_lb.astype(F32), axis=0)
    lb_cs = jnp.cumsum(lb_p, axis=0)
    lb = lb_cs - lb_cs[0]
    terms = jnp.stack([jnp.log(lb), jnp.log1p(-lb), 1.0 - lb], axis=2)
    return jnp.pad(terms, ((0, 0), (0, 0), (0, 5), (0, 0)))


def kernel(x_prompt, x_sample, c, state_rec, cache_k, cache_v, c_ctx, norm_w, ada_w, ada_b,
           rec_w_in, rec_lb, rec_gn_w, rec_w_out, attn_w_qkv, attn_qn_w, attn_kn_w, attn_sink,
           attn_w_out, peer_w_q, peer_keys, peer_u, peer_v):
    d = D_MODEL
    x = jnp.concatenate([x_prompt.reshape(CTX_TOKENS, d), x_sample.reshape(LAT_TOKENS, d)], axis=0)
    cond = jnp.zeros((N_GROUPS, d), F32).at[0].set(c_ctx).at[1:1 + DEC_BATCH].set(c)
    mods_all = _ada_mods(cond, ada_w, ada_b).reshape(DEPTH, N_GROUPS, N_MOD, d)

    lbp = _lower_bound_terms(rec_lb)
    rope = _rope_tables()
    ck = cache_k.reshape(DEC_BATCH, -1, PAST_LEN, KV_COLS)
    cv = cache_v.reshape(DEC_BATCH, -1, PAST_LEN, KV_COLS)

    new_rec, new_k, new_v = [], [], []
    for i in range(DEPTH):
        mods = mods_all[i]
        r = i // 2
        if i % 2 == 0:
            z = _norm_mod_matmul(x, norm_w[i, 0], mods, rec_w_in[r].astype(BF16), 0, 1, 1024)
            o_ctx, o_lat, states = [], [], []
            for backward in (False, True):
                o, st = _rec_scan(z, lbp[r], backward, 0, BATCH, SEQ, None, r, True)
                o_ctx.append(o)
                states.append(st)
                o_lat.extend(_rec_scan(z, lbp[r], backward, CTX_TOKENS, DEC_BATCH, DEC_SEQ, state_rec, r, False))
            gn = jnp.tile(rec_gn_w[r], REC_HEADS).reshape(1, d)
            x = _rec_out(o_ctx, o_lat, z, x, mods, gn, rec_w_out[r].astype(BF16), 2)
            new_rec.append(jnp.stack(states, axis=1))
        else:
            z = _norm_mod_matmul(x, norm_w[i, 0], mods, attn_w_qkv[r].astype(BF16), 0, 1, 512)
            wt = jnp.concatenate([jnp.tile(attn_qn_w[r], ATTN_HEADS), jnp.tile(attn_kn_w[r], ATTN_KV_HEADS)])
            qn, kn = _attn_prep(z, wt.reshape(1, QK_COLS), rope)
            a_ctx = _attn_ctx(attn_sink[r], qn, kn, z)
            a_lat = _attn_lat(attn_sink[r], qn, kn, z, ck, cv, r)
            x = _proj_residual(a_ctx, a_lat, x, mods, attn_w_out[r].astype(BF16), 2)
            new_k.append(kn[:CTX_TOKENS].reshape(BATCH, SEQ, ATTN_KV_HEADS, HEAD_DIM))
            new_v.append(z[:CTX_TOKENS, Q_COLS + KV_COLS:].reshape(BATCH, SEQ, ATTN_KV_HEADS, HEAD_DIM))
        wq_hi = peer_w_q[i].astype(BF16)
        wq_lo = (peer_w_q[i] - wq_hi.astype(F32)).astype(BF16)
        q, hb = _norm_mod_query(x, norm_w[i, 1], mods, wq_hi, wq_lo, 3, 4, 1024)
        k_hi = peer_keys[i].astype(BF16)
        k_lo = (peer_keys[i] - k_hi.astype(F32)).astype(BF16)
        r2, bw, nsel, aw = _peer_route(q, k_hi, k_lo)
        x = _peer_dense(hb, peer_u[i].astype(BF16), peer_v[i].T.astype(BF16), r2, bw, nsel, aw, x, mods, 5)

    y_prompt = x[:CTX_TOKENS].reshape(BATCH, SEQ, d)
    y_sample = x[CTX_TOKENS:].reshape(DEC_BATCH, DEC_SEQ, d)
    return (y_prompt, y_sample, jnp.stack(new_rec, axis=1), jnp.stack(new_k, axis=1), jnp.stack(new_v, axis=1))
```
